```python
import math
import jax, jax.numpy as jnp
from jax import lax
import numpy as np

D_MODEL = 1024
BATCH = 8
SEQ = 4096
DEPTH = 2

N_MIXERS = 2
N_GLA_LAYERS = (DEPTH + 1) // 2
N_DIFF_LAYERS = DEPTH // 2
BRANCH = 2 * D_MODEL
EPS = 1e-6

GLA_HEADS = 4
GLA_DK = D_MODEL // 2 // GLA_HEADS
GLA_DV = BRANCH // GLA_HEADS
GLA_RANK = 16
GLA_GATE_NORM = 16.0
GLA_CHUNK = 64
GLA_QK = GLA_HEADS * GLA_DK
GLA_IN = 2 * GLA_QK + 2 * BRANCH + GLA_RANK

DIFF_HEAD_DIM = 64
DIFF_HEADS = BRANCH // (2 * DIFF_HEAD_DIM)
DIFF_QBLOCK = 128
DIFF_IN = 4 * BRANCH
ROPE_THETA = 10000.0
LAMBDA_STD = 0.1

kernel_name = "hybrid_gla_diffattn_interleaved"


def rms_norm(x, g):
    xf = x.astype(jnp.float32)
    y = xf * lax.rsqrt(jnp.mean(xf * xf, axis=-1, keepdims=True) + EPS)
    return (y * g.astype(jnp.float32)).astype(x.dtype)


def rope_tables(T, d):
    inv_freq = 1.0 / (ROPE_THETA ** (jnp.arange(0, d, 2, dtype=jnp.float32) / d))
    pos = jnp.arange(T, dtype=jnp.float32)
    ang = pos[:, None] * inv_freq[None, :]
    return jnp.cos(ang), jnp.sin(ang)


def apply_rope(t, cos, sin):
    c = cos[:, None, None, :]
    s = sin[:, None, None, :]
    t1, t2 = jnp.split(t, 2, axis=-1)
    return jnp.concatenate([t1 * c - t2 * s, t2 * c + t1 * s], axis=-1)


def gla_mixer(h, w_in, w_g2, b_g, norm_g, w_out):
    B, T, _ = h.shape
    C = GLA_CHUNK
    N = T // C
    proj = h @ w_in
    q, k, v, gate, low = jnp.split(
        proj, [GLA_QK, 2 * GLA_QK, 2 * GLA_QK + BRANCH, 2 * GLA_QK + 2 * BRANCH], axis=-1)
    f32 = jnp.float32
    q = q.astype(f32).reshape(B, N, C, GLA_HEADS, GLA_DK) * (GLA_DK ** -0.5)
    k = k.astype(f32).reshape(B, N, C, GLA_HEADS, GLA_DK)
    v = v.astype(f32).reshape(B, N, C, GLA_HEADS, GLA_DV)
    logg = jax.nn.log_sigmoid((low @ w_g2 + b_g).astype(f32)) / GLA_GATE_NORM
    logg = logg.reshape(B, N, C, GLA_HEADS, GLA_DK)
    bcum = jnp.cumsum(logg, axis=2)
    blast = bcum[:, :, -1]
    q_t = q * jnp.exp(bcum)
    k_t = k * jnp.exp(-bcum)
    k_end = k * jnp.exp(blast[:, :, None] - bcum)
    causal = jnp.tril(jnp.ones((C, C), dtype=bool))
    A = jnp.einsum('bnihk,bnjhk->bnhij', q_t, k_t)
    A = jnp.where(causal, A, 0.0)
    o_intra = jnp.einsum('bnhij,bnjhv->bnihv', A, v)

    def step(S, inp):
        qn, kn, vn, dn = inp
        o = jnp.einsum('bihk,bhkv->bihv', qn, S)
        S = dn[..., None] * S + jnp.einsum('bjhk,bjhv->bhkv', kn, vn)
        return S, o

    xs = (jnp.moveaxis(q_t, 1, 0), jnp.moveaxis(k_end, 1, 0),
          jnp.moveaxis(v, 1, 0), jnp.moveaxis(jnp.exp(blast), 1, 0))
    S0 = jnp.zeros((B, GLA_HEADS, GLA_DK, GLA_DV), f32)
    _, o_inter = lax.scan(step, S0, xs)
    o = o_intra + jnp.moveaxis(o_inter, 0, 1)
    o = rms_norm(o.reshape(B, T, GLA_HEADS, GLA_DV), norm_g)
    o = o.reshape(B, T, BRANCH) * jax.nn.silu(gate.astype(f32))
    return o.astype(h.dtype) @ w_out


def diff_mixer(h, w_in, lq1, lk1, lq2, lk2, norm_g, w_out, lambda_init):
    B, T, _ = h.shape
    d = DIFF_HEAD_DIM
    f32 = jnp.float32
    proj = h @ w_in
    q, k, v, gate = jnp.split(proj, 4, axis=-1)
    cos, sin = rope_tables(T, d)
    q = apply_rope(q.astype(f32).reshape(B, T, DIFF_HEADS, 2, d), cos, sin)
    k = apply_rope(k.astype(f32).reshape(B, T, DIFF_HEADS, 2, d), cos, sin)
    v = v.astype(f32).reshape(B, T, DIFF_HEADS, 2 * d)
    lam = (jnp.exp(jnp.sum(lq1.astype(f32) * lk1.astype(f32)))
           - jnp.exp(jnp.sum(lq2.astype(f32) * lk2.astype(f32))) + lambda_init)
    qh = q.transpose(0, 2, 3, 1, 4)
    kh = k.transpose(0, 2, 3, 1, 4)
    vh = v.transpose(0, 2, 1, 3)
    scale = d ** -0.5
    outs = []
    for blk in range(T // DIFF_QBLOCK):
        s = blk * DIFF_QBLOCK
        e = s + DIFF_QBLOCK
        sc = jnp.einsum('bhcqd,bhckd->bhcqk', qh[:, :, :, s:e], kh[:, :, :, :e]) * scale
        causal = np.arange(e)[None, :] <= np.arange(s, e)[:, None]
        sc = jnp.where(causal, sc, -jnp.inf)
        p = jax.nn.softmax(sc, axis=-1)
        attn = p[:, :, 0] - lam * p[:, :, 1]
        outs.append(jnp.einsum('bhqk,bhkv->bhqv', attn, vh[:, :, :e]))
    o = jnp.concatenate(outs, axis=2)
    o = rms_norm(o, norm_g) * (1.0 - lambda_init)
    o = o.transpose(0, 2, 1, 3).reshape(B, T, BRANCH) * jax.nn.silu(gate.astype(f32))
    return o.astype(h.dtype) @ w_out


def setup_inputs(seed: int = 0) -> dict:
    key = jax.random.key(seed)
    ks = jax.random.split(key, 20)
    f32 = jnp.float32
    nrm = lambda k, shape, s: jax.random.normal(k, shape, f32) * s
    NG, ND = N_GLA_LAYERS, N_DIFF_LAYERS
    return {
        "x": jax.random.normal(ks[0], (BATCH, SEQ, D_MODEL), f32),
        "pre_g": 1.0 + nrm(ks[1], (DEPTH, D_MODEL), 0.02),
        "post_g": 1.0 + nrm(ks[2], (DEPTH, D_MODEL), 0.02),
        "gla_w_in": nrm(ks[3], (NG, D_MODEL, GLA_IN), D_MODEL ** -0.5),
        "gla_w_g2": nrm(ks[4], (NG, GLA_RANK, GLA_QK), GLA_RANK ** -0.5),
        "gla_b_g": nrm(ks[5], (NG, GLA_QK), 0.01),
        "gla_norm_g": 1.0 + nrm(ks[6], (NG, GLA_DV), 0.02),
        "gla_w_out": nrm(ks[7], (NG, BRANCH, D_MODEL), BRANCH ** -0.5),
        "diff_w_in": nrm(ks[8], (ND, D_MODEL, DIFF_IN), D_MODEL ** -0.5),
        "diff_lam_q1": nrm(ks[9], (ND, DIFF_HEAD_DIM), LAMBDA_STD),
        "diff_lam_k1": nrm(ks[10], (ND, DIFF_HEAD_DIM), LAMBDA_STD),
        "diff_lam_q2": nrm(ks[11], (ND, DIFF_HEAD_DIM), LAMBDA_STD),
        "diff_lam_k2": nrm(ks[12], (ND, DIFF_HEAD_DIM), LAMBDA_STD),
        "diff_norm_g": 1.0 + nrm(ks[13], (ND, 2 * DIFF_HEAD_DIM), 0.02),
        "diff_w_out": nrm(ks[14], (ND, BRANCH, D_MODEL), BRANCH ** -0.5),
    }


def reference(x, pre_g, post_g, gla_w_in, gla_w_g2, gla_b_g, gla_norm_g, gla_w_out,
              diff_w_in, diff_lam_q1, diff_lam_k1, diff_lam_q2, diff_lam_k2,
              diff_norm_g, diff_w_out):
    for i in range(DEPTH):
        h = rms_norm(x, pre_g[i])
        j = i // N_MIXERS
        if i % N_MIXERS == 0:
            y = gla_mixer(h, gla_w_in[j], gla_w_g2[j], gla_b_g[j], gla_norm_g[j], gla_w_out[j])
        else:
            lambda_init = 0.8 - 0.6 * math.exp(-0.3 * i)
            y = diff_mixer(h, diff_w_in[j], diff_lam_q1[j], diff_lam_k1[j], diff_lam_q2[j],
                           diff_lam_k2[j], diff_norm_g[j], diff_w_out[j], lambda_init)
        x = x + rms_norm(y, post_g[i])
    return x
```

```python
import functools
import math

import jax
import jax.numpy as jnp
from jax import lax
from jax.experimental import pallas as pl
from jax.experimental.pallas import tpu as pltpu

F32 = jnp.float32
BF16 = jnp.bfloat16

EPS = 1e-6
LANES = 128

GLA_HEADS = 4
GLA_DK = 128
GLA_DV = 512
GLA_RANK = 16
GLA_GATE_NORM = 16.0
GLA_CHUNK = 64
GLA_QK = GLA_HEADS * GLA_DK
GLA_BRANCH = GLA_HEADS * GLA_DV
GLA_MAIN = 2 * GLA_QK + 2 * GLA_BRANCH

DIFF_HEAD_DIM = 64
DIFF_HEADS = 16
DIFF_BRANCH = DIFF_HEADS * 2 * DIFF_HEAD_DIM
ROPE_THETA = 10000.0

VMEM_LIMIT = 48 * 1024 * 1024


def _dot(a, b):
    return jnp.dot(a, b, preferred_element_type=F32)


def _dot_nt(a, b):
    return lax.dot_general(a, b, (((1,), (1,)), ((), ())), preferred_element_type=F32)


def _dot_tn(a, b):
    return lax.dot_general(a, b, (((0,), (0,)), ((), ())), preferred_element_type=F32)


def _rms_scale(x):
    return lax.rsqrt(jnp.mean(x * x, axis=-1, keepdims=True) + EPS)


def _params(*sem):
    return pltpu.CompilerParams(dimension_semantics=sem, vmem_limit_bytes=VMEM_LIMIT)


def _gla_inproj_kernel(x_ref, g_ref, w_ref, wlow_ref, wg2_ref, bg_ref, o_ref, logg_ref, h_scr):
    @pl.when(pl.program_id(1) == 0)
    def _():
        x = x_ref[...]
        h = (x * _rms_scale(x) * g_ref[...]).astype(BF16)
        h_scr[...] = h
        low = _dot(h, wlow_ref[...])
        z = _dot(low.astype(BF16), wg2_ref[...]) + bg_ref[...]
        log_sig = jnp.minimum(z, 0.0) - jnp.log(1.0 + jnp.exp(-jnp.abs(z)))
        logg_ref[...] = log_sig / GLA_GATE_NORM

    o_ref[...] = _dot(h_scr[...], w_ref[...]).astype(o_ref.dtype)


def _gla_inproj(x, g, w_main, w_low, w_g2, b_g, *, tm=512, tn=1024):
    m, d = x.shape
    n = w_main.shape[1]
    return pl.pallas_call(
        _gla_inproj_kernel,
        grid=(m // tm, n // tn),
        in_specs=[
            pl.BlockSpec((tm, d), lambda i, j: (i, 0)),
            pl.BlockSpec((1, d), lambda i, j: (0, 0)),
            pl.BlockSpec((d, tn), lambda i, j: (0, j)),
            pl.BlockSpec((d, LANES), lambda i, j: (0, 0)),
            pl.BlockSpec((LANES, GLA_QK), lambda i, j: (0, 0)),
            pl.BlockSpec((1, GLA_QK), lambda i, j: (0, 0)),
        ],
        out_specs=[
            pl.BlockSpec((tm, tn), lambda i, j: (i, j)),
            pl.BlockSpec((tm, GLA_QK), lambda i, j: (i, 0)),
        ],
        out_shape=[
            jax.ShapeDtypeStruct((m, n), BF16),
            jax.ShapeDtypeStruct((m, GLA_QK), F32),
        ],
        scratch_shapes=[pltpu.VMEM((tm, d), BF16)],
        compiler_params=_params("parallel", "arbitrary"),
        name="gla_inproj",
    )(x, g, w_main, w_low, w_g2, b_g)


def _gla_core_kernel(q_ref, k_ref, v_ref, gate_ref, lg_ref, ng_ref, o_ref, st_scr, *, nchunk):
    @pl.when(pl.program_id(2) == 0)
    def _():
        st_scr[...] = jnp.zeros_like(st_scr)

    c = GLA_CHUNK
    row = lax.broadcasted_iota(jnp.int32, (c, c), 0)
    col = lax.broadcasted_iota(jnp.int32, (c, c), 1)
    causal = row >= col
    tri = jnp.where(causal, 1.0, 0.0).astype(BF16)

    for ci in range(nchunk):
        sl = pl.ds(ci * c, c)
        lg = lg_ref[sl, :]
        lg_hi = lg.astype(BF16)
        rem = lg - lg_hi.astype(F32)
        lg_mid = rem.astype(BF16)
        lg_lo = (rem - lg_mid.astype(F32)).astype(BF16)
        bcum = _dot(tri, lg_hi) + _dot(tri, lg_mid) + _dot(tri, lg_lo)
        blast = bcum[c - 1:c, :]

        q = q_ref[sl, :].astype(F32) * (GLA_DK ** -0.5)
        k = k_ref[sl, :].astype(F32)
        v = v_ref[sl, :]
        q_t = (q * jnp.exp(bcum)).astype(BF16)
        k_t = (k * jnp.exp(-bcum)).astype(BF16)
        k_end = (k * jnp.exp(blast - bcum)).astype(BF16)

        a = jnp.where(causal, _dot_nt(q_t, k_t), 0.0).astype(BF16)
        st = st_scr[...]
        o = _dot(a, v) + _dot_nt(q_t, st.astype(BF16))
        st_scr[...] = st * jnp.exp(blast) + _dot_tn(v, k_end)

        gate = gate_ref[sl, :].astype(F32)
        silu = gate / (1.0 + jnp.exp(-gate))
        o_ref[sl, :] = (o * _rms_scale(o) * ng_ref[...] * silu).astype(o_ref.dtype)


def _gla_core(proj, logg, norm_g, *, batch, seq, tc=256):
    m = proj.shape[0]
    nb = seq // tc
    qk_blocks = GLA_QK // GLA_DK
    v_block0 = 2 * GLA_QK // GLA_DV
    g_block0 = v_block0 + GLA_HEADS
    row = lambda b, h, t: b * nb + t
    return pl.pallas_call(
        functools.partial(_gla_core_kernel, nchunk=tc // GLA_CHUNK),
        grid=(batch, GLA_HEADS, nb),
        in_specs=[
            pl.BlockSpec((tc, GLA_DK), lambda b, h, t: (row(b, h, t), h)),
            pl.BlockSpec((tc, GLA_DK), lambda b, h, t: (row(b, h, t), qk_blocks + h)),
            pl.BlockSpec((tc, GLA_DV), lambda b, h, t: (row(b, h, t), v_block0 + h)),
            pl.BlockSpec((tc, GLA_DV), lambda b, h, t: (row(b, h, t), g_block0 + h)),
            pl.BlockSpec((tc, GLA_DK), lambda b, h, t: (row(b, h, t), h)),
            pl.BlockSpec((1, GLA_DV), lambda b, h, t: (0, 0)),
        ],
        out_specs=pl.BlockSpec((tc, GLA_DV), lambda b, h, t: (row(b, h, t), h)),
        out_shape=jax.ShapeDtypeStruct((m, GLA_BRANCH), BF16),
        scratch_shapes=[pltpu.VMEM((GLA_DV, GLA_DK), F32)],
        compiler_params=_params("parallel", "parallel", "arbitrary"),
        name="gla_core",
    )(proj, proj, proj, proj, logg, norm_g)


def _outproj_kernel(a_ref, w_ref, pg_ref, x_ref, o_ref):
    y = _dot(a_ref[...], w_ref[...])
    o_ref[...] = x_ref[...] + y * _rms_scale(y) * pg_ref[...]


def _outproj(a, w, post_g, x, *, tm=512):
    m, kdim = a.shape
    d = w.shape[1]
    return pl.pallas_call(
        _outproj_kernel,
        grid=(m // tm,),
        in_specs=[
            pl.BlockSpec((tm, kdim), lambda i: (i, 0)),
            pl.BlockSpec((kdim, d), lambda i: (0, 0)),
            pl.BlockSpec((1, d), lambda i: (0, 0)),
            pl.BlockSpec((tm, d), lambda i: (i, 0)),
        ],
        out_specs=pl.BlockSpec((tm, d), lambda i: (i, 0)),
        out_shape=jax.ShapeDtypeStruct((m, d), F32),
        compiler_params=_params("parallel"),
        name="outproj",
    )(a, w, post_g, x)


def _diff_inproj_kernel(x_ref, g_ref, w_ref, cos_ref, sin_ref, o_ref, h_scr, *, tn):
    j = pl.program_id(1)

    @pl.when(j == 0)
    def _():
        x = x_ref[...]
        h_scr[...] = (x * _rms_scale(x) * g_ref[...]).astype(BF16)

    y = _dot(h_scr[...], w_ref[...])
    n_q = DIFF_BRANCH // tn

    @pl.when(j < 2 * n_q)
    def _():
        scale = jnp.where(j < n_q, DIFF_HEAD_DIM ** -0.5, 1.0)
        cos = cos_ref[...] * scale
        sin = sin_ref[...] * scale
        lane = lax.broadcasted_iota(jnp.int32, cos.shape, 1)
        first_half = (lane % DIFF_HEAD_DIM) < (DIFF_HEAD_DIM // 2)
        for c in range(tn // LANES):
            sl = slice(c * LANES, (c + 1) * LANES)
            ysl = y[:, sl]
            partner = jnp.where(first_half,
                                pltpu.roll(ysl, LANES - DIFF_HEAD_DIM // 2, 1),
                                pltpu.roll(ysl, DIFF_HEAD_DIM // 2, 1))
            o_ref[:, sl] = (ysl * cos + partner * sin).astype(o_ref.dtype)

    @pl.when(j >= 2 * n_q)
    def _():
        o_ref[...] = y.astype(o_ref.dtype)


def _diff_inproj(x, g, w, cos, sin, *, seq, tm=512, tn=1024):
    m, d = x.shape
    n = w.shape[1]
    tblocks = seq // tm
    return pl.pallas_call(
        functools.partial(_diff_inproj_kernel, tn=tn),
        grid=(m // tm, n // tn),
        in_specs=[
            pl.BlockSpec((tm, d), lambda i, j: (i, 0)),
            pl.BlockSpec((1, d), lambda i, j: (0, 0)),
            pl.BlockSpec((d, tn), lambda i, j: (0, j)),
            pl.BlockSpec((tm, LANES), lambda i, j: (i % tblocks, 0)),
            pl.BlockSpec((tm, LANES), lambda i, j: (i % tblocks, 0)),
        ],
        out_specs=pl.BlockSpec((tm, tn), lambda i, j: (i, j)),
        out_shape=jax.ShapeDtypeStruct((m, n), BF16),
        scratch_shapes=[pltpu.VMEM((tm, d), BF16)],
        compiler_params=_params("parallel", "arbitrary"),
        name="diff_inproj",
    )(x, g, w, cos, sin)


def _rope_tables(seq):
    d = DIFF_HEAD_DIM
    inv_freq = 1.0 / (ROPE_THETA ** (jnp.arange(0, d, 2, dtype=F32) / d))
    ang = jnp.arange(seq, dtype=F32)[:, None] * inv_freq[None, :]
    cos, sin = jnp.cos(ang), jnp.sin(ang)
    cos_l = jnp.concatenate([cos, cos, cos, cos], axis=-1)
    sin_l = jnp.concatenate([-sin, sin, -sin, sin], axis=-1)
    return cos_l, sin_l


def _diff_attn_kernel(lq1_ref, lk1_ref, lq2_ref, lk2_ref, q_ref, k_ref, v_ref, gate_ref, ng_ref, o_ref,
                      qs_scr, m_scr, l_scr, acc_scr, *, tq, tk, lambda_init):
    qi = pl.program_id(2)
    d = DIFF_HEAD_DIM

    q = q_ref[...]
    lane = lax.broadcasted_iota(jnp.int32, q.shape, 1)
    qs_scr[0:tq, :] = jnp.where(lane < d, q, jnp.zeros_like(q))
    qs_scr[tq:2 * tq, :] = jnp.where(lane >= d, q, jnp.zeros_like(q))
    m_scr[...] = jnp.full_like(m_scr, -jnp.inf)
    l_scr[...] = jnp.zeros_like(l_scr)
    acc_scr[...] = jnp.zeros_like(acc_scr)

    def block(kv, masked):
        start = pl.multiple_of(kv * tk, tk)
        k = k_ref[pl.ds(start, tk), :]
        v = v_ref[pl.ds(start, tk), :]
        s = _dot_nt(qs_scr[...], k)
        if masked:
            qpos = qi * tq + lax.broadcasted_iota(jnp.int32, (2 * tq, tk), 0) % tq
            kpos = start + lax.broadcasted_iota(jnp.int32, (2 * tq, tk), 1)
            s = jnp.where(kpos <= qpos, s, -jnp.inf)
        m_prev = m_scr[...]
        m_new = jnp.maximum(m_prev, jnp.max(s, axis=-1, keepdims=True))
        alpha = jnp.exp(m_prev - m_new)
        p = jnp.exp(s - m_new)
        l_scr[...] = alpha * l_scr[...] + jnp.sum(p, axis=-1, keepdims=True)
        acc_scr[...] = alpha * acc_scr[...] + _dot(p.astype(BF16), v)
        m_scr[...] = m_new

    n_full = qi * (tq // tk)

    def full_body(kv, carry):
        block(kv, masked=False)
        return carry

    lax.fori_loop(0, n_full, full_body, 0)
    for t in range(tq // tk):
        block(n_full + t, masked=True)

    lam = (jnp.exp(jnp.sum(lq1_ref[...] * lk1_ref[...])) - jnp.exp(jnp.sum(lq2_ref[...] * lk2_ref[...]))
           + lambda_init)
    o1 = acc_scr[0:tq, :] / l_scr[0:tq, :]
    o2 = acc_scr[tq:2 * tq, :] / l_scr[tq:2 * tq, :]
    o = o1 - lam * o2
    o = o * _rms_scale(o) * ng_ref[...] * (1.0 - lambda_init)
    gate = gate_ref[...].astype(F32)
    silu = gate / (1.0 + jnp.exp(-gate))
    o_ref[...] = (o * silu).astype(o_ref.dtype)


def _diff_attn(proj, lq1, lk1, lq2, lk2, norm_g, lambda_init, *, batch, seq, tq=512, tk=512):
    m = proj.shape[0]
    nq = seq // tq
    hd = 2 * DIFF_HEAD_DIM
    vec = pl.BlockSpec((1, DIFF_HEAD_DIM), lambda b, h, i: (0, 0))
    return pl.pallas_call(
        functools.partial(_diff_attn_kernel, tq=tq, tk=tk, lambda_init=lambda_init),
        grid=(batch, DIFF_HEADS, nq),
        in_specs=[
            vec, vec, vec, vec,
            pl.BlockSpec((tq, hd), lambda b, h, i: (b * nq + i, h)),
            pl.BlockSpec((seq, hd), lambda b, h, i: (b, DIFF_HEADS + h)),
            pl.BlockSpec((seq, hd), lambda b, h, i: (b, 2 * DIFF_HEADS + h)),
            pl.BlockSpec((tq, hd), lambda b, h, i: (b * nq + i, 3 * DIFF_HEADS + h)),
            pl.BlockSpec((1, hd), lambda b, h, i: (0, 0)),
        ],
        out_specs=pl.BlockSpec((tq, hd), lambda b, h, i: (b * nq + i, h)),
        out_shape=jax.ShapeDtypeStruct((m, DIFF_BRANCH), BF16),
        scratch_shapes=[
            pltpu.VMEM((2 * tq, hd), BF16),
            pltpu.VMEM((2 * tq, 1), F32),
            pltpu.VMEM((2 * tq, 1), F32),
            pltpu.VMEM((2 * tq, hd), F32),
        ],
        compiler_params=_params("parallel", "parallel", "arbitrary"),
        name="diff_attn",
    )(lq1, lk1, lq2, lk2, proj, proj, proj, proj, norm_g)


def _gla_layer(x, pre_g, post_g, w_in, w_g2, b_g, norm_g, w_out, *, batch, seq):
    w_in = w_in.astype(BF16)
    w_main = w_in[:, :GLA_MAIN]
    w_low = jnp.pad(w_in[:, GLA_MAIN:], ((0, 0), (0, LANES - GLA_RANK)))
    w_g2p = jnp.pad(w_g2.astype(BF16), ((0, LANES - GLA_RANK), (0, 0)))
    proj, logg = _gla_inproj(x, pre_g[None, :], w_main, w_low, w_g2p, b_g[None, :])
    og = _gla_core(proj, logg, norm_g[None, :], batch=batch, seq=seq)
    return _outproj(og, w_out.astype(BF16), post_g[None, :], x)


def _diff_layer(x, pre_g, post_g, w_in, lq1, lk1, lq2, lk2, norm_g, w_out, lambda_init, *, batch, seq):
    cos, sin = _rope_tables(seq)
    proj = _diff_inproj(x, pre_g[None, :], w_in.astype(BF16), cos, sin, seq=seq)
    og = _diff_attn(proj, lq1[None, :], lk1[None, :], lq2[None, :], lk2[None, :], norm_g[None, :],
                    lambda_init, batch=batch, seq=seq)
    return _outproj(og, w_out.astype(BF16), post_g[None, :], x)


def kernel(x, pre_g, post_g, gla_w_in, gla_w_g2, gla_b_g, gla_norm_g, gla_w_out, diff_w_in,
           diff_lam_q1, diff_lam_k1, diff_lam_q2, diff_lam_k2, diff_norm_g, diff_w_out):
    batch, seq, d_model = x.shape
    depth = pre_g.shape[0]
    xf = x.reshape(batch * seq, d_model)
    for i in range(depth):
        j = i // 2
        if i % 2 == 0:
            xf = _gla_layer(xf, pre_g[i], post_g[i], gla_w_in[j], gla_w_g2[j], gla_b_g[j], gla_norm_g[j],
                            gla_w_out[j], batch=batch, seq=seq)
        else:
            lambda_init = 0.8 - 0.6 * math.exp(-0.3 * i)
            xf = _diff_layer(xf, pre_g[i], post_g[i], diff_w_in[j], diff_lam_q1[j], diff_lam_k1[j],
                             diff_lam_q2[j], diff_lam_k2[j], diff_norm_g[j], diff_w_out[j], lambda_init,
                             batch=batch, seq=seq)
    return xf.reshape(batch, seq, d_model)
```

```python
import functools
import math

import jax
import jax.numpy as jnp
from jax import lax
from jax.experimental import pallas as pl
from jax.experimental.pallas import tpu as pltpu

F32 = jnp.float32
BF16 = jnp.bfloat16

EPS = 1e-6
LANES = 128

GLA_HEADS = 4
GLA_DK = 128
GLA_DV = 512
GLA_RANK = 16
GLA_GATE_NORM = 16.0
GLA_CHUNK = 64
GLA_QK = GLA_HEADS * GLA_DK
GLA_BRANCH = GLA_HEADS * GLA_DV
GLA_MAIN = 2 * GLA_QK + 2 * GLA_BRANCH

DIFF_HEAD_DIM = 64
DIFF_HEADS = 16
DIFF_BRANCH = DIFF_HEADS * 2 * DIFF_HEAD_DIM
DIFF_KBLOCK = 256
DIFF_Q_SCALE = DIFF_HEAD_DIM ** -0.5 * math.log2(math.e)
ROPE_THETA = 10000.0

VMEM_LIMIT = 48 * 1024 * 1024


def _dot(a, b):
    return jnp.dot(a, b, preferred_element_type=F32)


def _dot_nt(a, b):
    return lax.dot_general(a, b, (((1,), (1,)), ((), ())), preferred_element_type=F32)


def _dot_tn(a, b):
    return lax.dot_general(a, b, (((0,), (0,)), ((), ())), preferred_element_type=F32)


def _rms_scale(x):
    return lax.rsqrt(jnp.mean(x * x, axis=-1, keepdims=True) + EPS)


def _params(*sem):
    return pltpu.CompilerParams(dimension_semantics=sem, vmem_limit_bytes=VMEM_LIMIT)


def _gla_inproj_kernel(x_ref, g_ref, w_ref, wlow_ref, wg2_ref, bg_ref, o_ref, logg_ref, h_scr):
    @pl.when(pl.program_id(1) == 0)
    def _():
        x = x_ref[...]
        h = (x * _rms_scale(x) * g_ref[...]).astype(BF16)
        h_scr[...] = h
        low = _dot(h, wlow_ref[...])
        z = _dot(low.astype(BF16), wg2_ref[...]) + bg_ref[...]
        log_sig = jnp.minimum(z, 0.0) - jnp.log(1.0 + jnp.exp(-jnp.abs(z)))
        logg_ref[...] = log_sig / GLA_GATE_NORM

    o_ref[...] = _dot(h_scr[...], w_ref[...]).astype(o_ref.dtype)


def _gla_inproj(x, g, w_main, w_low, w_g2, b_g, *, tm=512, tn=1024):
    m, d = x.shape
    n = w_main.shape[1]
    return pl.pallas_call(
        _gla_inproj_kernel,
        grid=(m // tm, n // tn),
        in_specs=[
            pl.BlockSpec((tm, d), lambda i, j: (i, 0)),
            pl.BlockSpec((1, d), lambda i, j: (0, 0)),
            pl.BlockSpec((d, tn), lambda i, j: (0, j)),
            pl.BlockSpec((d, LANES), lambda i, j: (0, 0)),
            pl.BlockSpec((LANES, GLA_QK), lambda i, j: (0, 0)),
            pl.BlockSpec((1, GLA_QK), lambda i, j: (0, 0)),
        ],
        out_specs=[
            pl.BlockSpec((tm, tn), lambda i, j: (i, j)),
            pl.BlockSpec((tm, GLA_QK), lambda i, j: (i, 0)),
        ],
        out_shape=[
            jax.ShapeDtypeStruct((m, n), BF16),
            jax.ShapeDtypeStruct((m, GLA_QK), F32),
        ],
        scratch_shapes=[pltpu.VMEM((tm, d), BF16)],
        compiler_params=_params("parallel", "arbitrary"),
        name="gla_inproj",
    )(x, g, w_main, w_low, w_g2, b_g)


def _gla_core_kernel(q_ref, k_ref, v_ref, gate_ref, lg_ref, ng_ref, o_ref, st_scr, *, nchunk):
    @pl.when(pl.program_id(2) == 0)
    def _():
        st_scr[...] = jnp.zeros_like(st_scr)

    c = GLA_CHUNK
    row = lax.broadcasted_iota(jnp.int32, (c, c), 0)
    col = lax.broadcasted_iota(jnp.int32, (c, c), 1)
    causal = row >= col
    tri = jnp.where(causal, 1.0, 0.0).astype(BF16)

    for ci in range(nchunk):
        sl = pl.ds(ci * c, c)
        lg = lg_ref[sl, :]
        lg_hi = lg.astype(BF16)
        rem = lg - lg_hi.astype(F32)
        lg_mid = rem.astype(BF16)
        lg_lo = (rem - lg_mid.astype(F32)).astype(BF16)
        bcum = _dot(tri, lg_hi) + _dot(tri, lg_mid) + _dot(tri, lg_lo)
        blast = bcum[c - 1:c, :]

        q = q_ref[sl, :].astype(F32) * (GLA_DK ** -0.5)
        k = k_ref[sl, :].astype(F32)
        v = v_ref[sl, :]
        q_t = (q * jnp.exp(bcum)).astype(BF16)
        k_t = (k * jnp.exp(-bcum)).astype(BF16)
        k_end = (k * jnp.exp(blast - bcum)).astype(BF16)

        a = jnp.where(causal, _dot_nt(q_t, k_t), 0.0).astype(BF16)
        st = st_scr[...]
        o = _dot(a, v) + _dot_nt(q_t, st.astype(BF16))
        st_scr[...] = st * jnp.exp(blast) + _dot_tn(v, k_end)

        gate = gate_ref[sl, :].astype(F32)
        silu = gate / (1.0 + jnp.exp(-gate))
        o_ref[sl, :] = (o * _rms_scale(o) * ng_ref[...] * silu).astype(o_ref.dtype)


def _gla_core(proj, logg, norm_g, *, batch, seq, tc=256):
    m = proj.shape[0]
    nb = seq // tc
    qk_blocks = GLA_QK // GLA_DK
    v_block0 = 2 * GLA_QK // GLA_DV
    g_block0 = v_block0 + GLA_HEADS
    row = lambda b, h, t: b * nb + t
    return pl.pallas_call(
        functools.partial(_gla_core_kernel, nchunk=tc // GLA_CHUNK),
        grid=(batch, GLA_HEADS, nb),
        in_specs=[
            pl.BlockSpec((tc, GLA_DK), lambda b, h, t: (row(b, h, t), h)),
            pl.BlockSpec((tc, GLA_DK), lambda b, h, t: (row(b, h, t), qk_blocks + h)),
            pl.BlockSpec((tc, GLA_DV), lambda b, h, t: (row(b, h, t), v_block0 + h)),
            pl.BlockSpec((tc, GLA_DV), lambda b, h, t: (row(b, h, t), g_block0 + h)),
            pl.BlockSpec((tc, GLA_DK), lambda b, h, t: (row(b, h, t), h)),
            pl.BlockSpec((1, GLA_DV), lambda b, h, t: (0, 0)),
        ],
        out_specs=pl.BlockSpec((tc, GLA_DV), lambda b, h, t: (row(b, h, t), h)),
        out_shape=jax.ShapeDtypeStruct((m, GLA_BRANCH), BF16),
        scratch_shapes=[pltpu.VMEM((GLA_DV, GLA_DK), F32)],
        compiler_params=_params("parallel", "parallel", "arbitrary"),
        name="gla_core",
    )(proj, proj, proj, proj, logg, norm_g)


def _outproj_kernel(a_ref, w_ref, pg_ref, x_ref, o_ref):
    y = _dot(a_ref[...], w_ref[...])
    o_ref[...] = x_ref[...] + y * _rms_scale(y) * pg_ref[...]


def _outproj(a, w, post_g, x, *, tm=512):
    m, kdim = a.shape
    d = w.shape[1]
    return pl.pallas_call(
        _outproj_kernel,
        grid=(m // tm,),
        in_specs=[
            pl.BlockSpec((tm, kdim), lambda i: (i, 0)),
            pl.BlockSpec((kdim, d), lambda i: (0, 0)),
            pl.BlockSpec((1, d), lambda i: (0, 0)),
            pl.BlockSpec((tm, d), lambda i: (i, 0)),
        ],
        out_specs=pl.BlockSpec((tm, d), lambda i: (i, 0)),
        out_shape=jax.ShapeDtypeStruct((m, d), F32),
        compiler_params=_params("parallel"),
        name="outproj",
    )(a, w, post_g, x)


def _diff_inproj_kernel(x_ref, g_ref, w_ref, cos_ref, sin_ref, o_ref, h_scr, *, tn):
    j = pl.program_id(1)

    @pl.when(j == 0)
    def _():
        x = x_ref[...]
        h_scr[...] = (x * _rms_scale(x) * g_ref[...]).astype(BF16)

    y = _dot(h_scr[...], w_ref[...])
    n_q = DIFF_BRANCH // tn

    @pl.when(j < 2 * n_q)
    def _():
        scale = jnp.where(j < n_q, DIFF_Q_SCALE, 1.0)
        cos = cos_ref[...] * scale
        sin = sin_ref[...] * scale
        lane = lax.broadcasted_iota(jnp.int32, cos.shape, 1)
        first_half = (lane % DIFF_HEAD_DIM) < (DIFF_HEAD_DIM // 2)
        for c in range(tn // LANES):
            sl = slice(c * LANES, (c + 1) * LANES)
            ysl = y[:, sl]
            partner = jnp.where(first_half,
                                pltpu.roll(ysl, LANES - DIFF_HEAD_DIM // 2, 1),
                                pltpu.roll(ysl, DIFF_HEAD_DIM // 2, 1))
            o_ref[:, sl] = (ysl * cos + partner * sin).astype(o_ref.dtype)

    @pl.when(j >= 2 * n_q)
    def _():
        o_ref[...] = y.astype(o_ref.dtype)


def _diff_inproj(x, g, w, cos, sin, *, seq, tm=512, tn=1024):
    m, d = x.shape
    n = w.shape[1]
    tblocks = seq // tm
    return pl.pallas_call(
        functools.partial(_diff_inproj_kernel, tn=tn),
        grid=(m // tm, n // tn),
        in_specs=[
            pl.BlockSpec((tm, d), lambda i, j: (i, 0)),
            pl.BlockSpec((1, d), lambda i, j: (0, 0)),
            pl.BlockSpec((d, tn), lambda i, j: (0, j)),
            pl.BlockSpec((tm, LANES), lambda i, j: (i % tblocks, 0)),
            pl.BlockSpec((tm, LANES), lambda i, j: (i % tblocks, 0)),
        ],
        out_specs=pl.BlockSpec((tm, tn), lambda i, j: (i, j)),
        out_shape=jax.ShapeDtypeStruct((m, n), BF16),
        scratch_shapes=[pltpu.VMEM((tm, d), BF16)],
        compiler_params=_params("parallel", "arbitrary"),
        name="diff_inproj",
    )(x, g, w, cos, sin)


def _rope_tables(seq):
    d = DIFF_HEAD_DIM
    inv_freq = 1.0 / (ROPE_THETA ** (jnp.arange(0, d, 2, dtype=F32) / d))
    ang = jnp.arange(seq, dtype=F32)[:, None] * inv_freq[None, :]
    cos, sin = jnp.cos(ang), jnp.sin(ang)
    cos_l = jnp.concatenate([cos, cos, cos, cos], axis=-1)
    sin_l = jnp.concatenate([-sin, sin, -sin, sin], axis=-1)
    return cos_l, sin_l


def _diff_attn_kernel(lq1_ref, lk1_ref, lq2_ref, lk2_ref, q_ref, k_ref, v_ref, gate_ref, ng_ref, o_ref,
                      vt_scr, qs_scr, s_scr, m_scr, l_scr, acc_scr, *, tq, seq, lambda_init):
    qi = pl.program_id(2)
    d = DIFF_HEAD_DIM
    tk = DIFF_KBLOCK
    assert tq == 2 * tk
    map1, map2 = slice(0, tq), slice(tq, 2 * tq)
    map1_hi, map2_hi = slice(tk, tq), slice(tq + tk, 2 * tq)

    @pl.when(qi == 0)
    def _():
        for r in range(seq // tk):
            vt_scr[r] = v_ref[r * tk:(r + 1) * tk, :].astype(F32).T.astype(BF16)

    qt = q_ref[...].astype(F32).T
    row = lax.broadcasted_iota(jnp.int32, qt.shape, 0)
    qs_scr[:, map1] = jnp.where(row < d, qt, 0.0).astype(BF16)
    qs_scr[:, map2] = jnp.where(row >= d, qt, 0.0).astype(BF16)
    m_scr[...] = jnp.full_like(m_scr, -jnp.inf)
    l_scr[...] = jnp.zeros_like(l_scr)
    acc_scr[...] = jnp.zeros_like(acc_scr)

    def scores(kstart, cs):
        return _dot(k_ref[pl.ds(kstart, tk), :], qs_scr[:, cs])

    def causal(s):
        kpos = lax.broadcasted_iota(jnp.int32, s.shape, 0)
        qpos = lax.broadcasted_iota(jnp.int32, s.shape, 1)
        return jnp.where(kpos <= qpos, s, -jnp.inf)

    def fold(s, cs, vt):
        m_old = m_scr[:, cs]
        m_new = jnp.maximum(m_old, jnp.max(s, axis=0, keepdims=True))
        alpha = jnp.exp2(m_old - m_new)
        p = jnp.exp2(s - m_new)
        l_scr[:, cs] = alpha * l_scr[:, cs] + jnp.sum(p, axis=0, keepdims=True)
        acc_scr[:, cs] = alpha * acc_scr[:, cs] + _dot(vt, p.astype(BF16))
        m_scr[:, cs] = m_new

    s_scr[...] = scores(0, map1)

    def full_body(jj, carry):
        s1 = s_scr[...]
        for u in range(tq // tk):
            j = jj * (tq // tk) + u
            kstart = pl.multiple_of(j * tk, tk)
            vt = vt_scr[j]
            s2 = scores(kstart, map2)
            fold(s1, map1, vt)
            s1 = scores(kstart + tk, map1)
            fold(s2, map2, vt)
        s_scr[...] = s1
        return carry

    n_full = qi * (tq // tk)
    lax.fori_loop(0, qi, full_body, 0)

    d0 = pl.multiple_of(qi * tq, tq)
    vt0 = vt_scr[n_full]
    vt1 = vt_scr[n_full + 1]
    s2 = scores(d0, map2)
    fold(causal(s_scr[...]), map1, vt0)
    s3 = scores(d0 + tk, map1_hi)
    fold(causal(s2), map2, vt0)
    s4 = scores(d0 + tk, map2_hi)
    fold(causal(s3), map1_hi, vt1)
    fold(causal(s4), map2_hi, vt1)

    lam = (jnp.exp(jnp.sum(lq1_ref[...] * lk1_ref[...])) - jnp.exp(jnp.sum(lq2_ref[...] * lk2_ref[...]))
           + lambda_init)
    on = acc_scr[...] * (1.0 / l_scr[...])
    o = (on[:, map1] - lam * on[:, map2]).T
    o = o * _rms_scale(o) * ng_ref[...] * (1.0 - lambda_init)
    gate = gate_ref[...].astype(F32)
    silu = gate / (1.0 + jnp.exp(-gate))
    o_ref[...] = (o * silu).astype(o_ref.dtype)


def _diff_attn(proj, lq1, lk1, lq2, lk2, norm_g, lambda_init, *, batch, seq, tq=512):
    m = proj.shape[0]
    nq = seq // tq
    hd = 2 * DIFF_HEAD_DIM
    vec = pl.BlockSpec((1, DIFF_HEAD_DIM), lambda b, h, i: (0, 0))
    return pl.pallas_call(
        functools.partial(_diff_attn_kernel, tq=tq, seq=seq, lambda_init=lambda_init),
        grid=(batch, DIFF_HEADS, nq),
        in_specs=[
            vec, vec, vec, vec,
            pl.BlockSpec((tq, hd), lambda b, h, i: (b * nq + i, h)),
            pl.BlockSpec((seq, hd), lambda b, h, i: (b, DIFF_HEADS + h)),
            pl.BlockSpec((seq, hd), lambda b, h, i: (b, 2 * DIFF_HEADS + h)),
            pl.BlockSpec((tq, hd), lambda b, h, i: (b * nq + i, 3 * DIFF_HEADS + h)),
            pl.BlockSpec((1, hd), lambda b, h, i: (0, 0)),
        ],
        out_specs=pl.BlockSpec((tq, hd), lambda b, h, i: (b * nq + i, h)),
        out_shape=jax.ShapeDtypeStruct((m, DIFF_BRANCH), BF16),
        scratch_shapes=[
            pltpu.VMEM((seq // DIFF_KBLOCK, hd, DIFF_KBLOCK), BF16),
            pltpu.VMEM((hd, 2 * tq), BF16),
            pltpu.VMEM((DIFF_KBLOCK, tq), F32),
            pltpu.VMEM((1, 2 * tq), F32),
            pltpu.VMEM((1, 2 * tq), F32),
            pltpu.VMEM((hd, 2 * tq), F32),
        ],
        compiler_params=_params("parallel", "parallel", "arbitrary"),
        name="diff_attn",
    )(lq1, lk1, lq2, lk2, proj, proj, proj, proj, norm_g)


def _gla_layer(x, pre_g, post_g, w_in, w_g2, b_g, norm_g, w_out, *, batch, seq):
    w_in = w_in.astype(BF16)
    w_main = w_in[:, :GLA_MAIN]
    w_low = jnp.pad(w_in[:, GLA_MAIN:], ((0, 0), (0, LANES - GLA_RANK)))
    w_g2p = jnp.pad(w_g2.astype(BF16), ((0, LANES - GLA_RANK), (0, 0)))
    proj, logg = _gla_inproj(x, pre_g[None, :], w_main, w_low, w_g2p, b_g[None, :])
    og = _gla_core(proj, logg, norm_g[None, :], batch=batch, seq=seq)
    return _outproj(og, w_out.astype(BF16), post_g[None, :], x)


def _diff_layer(x, pre_g, post_g, w_in, lq1, lk1, lq2, lk2, norm_g, w_out, lambda_init, *, batch, seq):
    cos, sin = _rope_tables(seq)
    proj = _diff_inproj(x, pre_g[None, :], w_in.astype(BF16), cos, sin, seq=seq)
    og = _diff_attn(proj, lq1[None, :], lk1[None, :], lq2[None, :], lk2[None, :], norm_g[None, :],
                    lambda_init, batch=batch, seq=seq)
    return _outproj(og, w_out.astype(BF16), post_g[None, :], x)


def kernel(x, pre_g, post_g, gla_w_in, gla_w_g2, gla_b_g, gla_norm_g, gla_w_out, diff_w_in,
           diff_lam_q1, diff_lam_k1, diff_lam_q2, diff_lam_k2, diff_norm_g, diff_w_out):
    batch, seq, d_model = x.shape
    depth = pre_g.shape[0]
    xf = x.reshape(batch * seq, d_model)
    for i in range(depth):
        j = i // 2
        if i % 2 == 0:
            xf = _gla_layer(xf, pre_g[i], post_g[i], gla_w_in[j], gla_w_g2[j], gla_b_g[j], gla_norm_g[j],
                            gla_w_out[j], batch=batch, seq=seq)
        else:
            lambda_init = 0.8 - 0.6 * math.exp(-0.3 * i)
            xf = _diff_layer(xf, pre_g[i], post_g[i], diff_w_in[j], diff_lam_q1[j], diff_lam_k1[j],
                             diff_lam_q2[j], diff_lam_k2[j], diff_norm_g[j], diff_w_out[j], lambda_init,
                             batch=batch, seq=seq)
    return xf.reshape(batch, seq, d_model)
```

```python
import functools
import math

import jax
import jax.numpy as jnp
from jax import lax
from jax.experimental import pallas as pl
from jax.experimental.pallas import tpu as pltpu

F32 = jnp.float32
BF16 = jnp.bfloat16

EPS = 1e-6
LANES = 128

GLA_HEADS = 4
GLA_DK = 128
GLA_DV = 512
GLA_RANK = 16
GLA_GATE_NORM = 16.0
GLA_CHUNK = 64
GLA_QK = GLA_HEADS * GLA_DK
GLA_BRANCH = GLA_HEADS * GLA_DV
GLA_MAIN = 2 * GLA_QK + 2 * GLA_BRANCH

DIFF_HEAD_DIM = 64
DIFF_HEADS = 16
DIFF_BRANCH = DIFF_HEADS * 2 * DIFF_HEAD_DIM
DIFF_KBLOCK = 256
DIFF_Q_SCALE = DIFF_HEAD_DIM ** -0.5 * math.log2(math.e)
ROPE_THETA = 10000.0

VMEM_LIMIT = 48 * 1024 * 1024


def _dot(a, b):
    return jnp.dot(a, b, preferred_element_type=F32)


def _dot_nt(a, b):
    return lax.dot_general(a, b, (((1,), (1,)), ((), ())), preferred_element_type=F32)


def _dot_tn(a, b):
    return lax.dot_general(a, b, (((0,), (0,)), ((), ())), preferred_element_type=F32)


def _rms_scale(x):
    return lax.rsqrt(jnp.mean(x * x, axis=-1, keepdims=True) + EPS)


def _params(*sem):
    return pltpu.CompilerParams(dimension_semantics=sem, vmem_limit_bytes=VMEM_LIMIT)


def _gla_inproj_kernel(x_ref, g_ref, w_ref, wlow_ref, wg2_ref, bg_ref, o_ref, logg_ref, *, tn):
    x = x_ref[...]
    h = (x * _rms_scale(x) * g_ref[...]).astype(BF16)
    for c in range(o_ref.shape[1] // tn):
        cs = slice(c * tn, (c + 1) * tn)
        o_ref[:, cs] = _dot(h, w_ref[:, cs]).astype(o_ref.dtype)
    low = _dot(h, wlow_ref[...])
    z = _dot(low.astype(BF16), wg2_ref[...]) + bg_ref[...]
    log_sig = jnp.minimum(z, 0.0) - jnp.log(1.0 + jnp.exp(-jnp.abs(z)))
    logg_ref[...] = log_sig / GLA_GATE_NORM


def _resident(shape):
    return pl.BlockSpec(shape, lambda i: (0,) * len(shape), pipeline_mode=pl.Buffered(1))


def _gla_inproj(x, g, w_main, w_low, w_g2, b_g, *, tm=512, tn=512):
    m, d = x.shape
    n = w_main.shape[1]
    return pl.pallas_call(
        functools.partial(_gla_inproj_kernel, tn=tn),
        grid=(m // tm,),
        in_specs=[
            pl.BlockSpec((tm, d), lambda i: (i, 0)),
            _resident((1, d)),
            _resident((d, n)),
            _resident((d, LANES)),
            _resident((LANES, GLA_QK)),
            _resident((1, GLA_QK)),
        ],
        out_specs=[
            pl.BlockSpec((tm, n), lambda i: (i, 0)),
            pl.BlockSpec((tm, GLA_QK), lambda i: (i, 0)),
        ],
        out_shape=[
            jax.ShapeDtypeStruct((m, n), BF16),
            jax.ShapeDtypeStruct((m, GLA_QK), F32),
        ],
        compiler_params=_params("parallel"),
        name="gla_inproj",
    )(x, g, w_main, w_low, w_g2, b_g)


def _gla_core_kernel(q_ref, k_ref, v_ref, gate_ref, lg_ref, ng_ref, o_ref, st_scr, *, nchunk):
    @pl.when(pl.program_id(2) == 0)
    def _():
        st_scr[...] = jnp.zeros_like(st_scr)

    c = GLA_CHUNK
    tc = nchunk * c
    chunks = [slice(n * c, (n + 1) * c) for n in range(nchunk)]
    row = lax.broadcasted_iota(jnp.int32, (tc, tc), 0)
    col = lax.broadcasted_iota(jnp.int32, (tc, tc), 1)
    causal = (row >= col) & (row // c == col // c)
    tri = jnp.where(causal, 1.0, 0.0).astype(BF16)

    lg = lg_ref[...]
    lg_hi = lg.astype(BF16)
    rem = lg - lg_hi.astype(F32)
    lg_mid = rem.astype(BF16)
    lg_lo = (rem - lg_mid.astype(F32)).astype(BF16)
    bcum = _dot(tri, lg_hi) + _dot(tri, lg_mid) + _dot(tri, lg_lo)
    blast = [bcum[(n + 1) * c - 1:(n + 1) * c, :] for n in range(nchunk)]
    blast_rows = jnp.concatenate([jnp.broadcast_to(b, (c, GLA_DK)) for b in blast], axis=0)

    q = q_ref[...].astype(F32) * (GLA_DK ** -0.5)
    k = k_ref[...].astype(F32)
    v = v_ref[...]
    q_t = (q * jnp.exp(bcum)).astype(BF16)
    k_t = (k * jnp.exp(-bcum)).astype(BF16)
    k_end = (k * jnp.exp(blast_rows - bcum)).astype(BF16)

    a = jnp.where(causal, _dot_nt(q_t, k_t), 0.0).astype(BF16)
    o_intra = _dot(a, v)
    kv = [_dot_tn(v[sl, :], k_end[sl, :]) for sl in chunks]

    st = st_scr[...]
    o_inter = []
    for n, sl in enumerate(chunks):
        o_inter.append(_dot_nt(q_t[sl, :], st.astype(BF16)))
        st = st * jnp.exp(blast[n]) + kv[n]
    st_scr[...] = st

    o = o_intra + jnp.concatenate(o_inter, axis=0)
    gate = gate_ref[...].astype(F32)
    silu = gate * (0.5 + 0.5 * jnp.tanh(0.5 * gate))
    o_ref[...] = (o * _rms_scale(o) * ng_ref[...] * silu).astype(o_ref.dtype)


def _gla_core(proj, logg, norm_g, *, batch, seq, tc=256):
    m = proj.shape[0]
    nb = seq // tc
    qk_blocks = GLA_QK // GLA_DK
    v_block0 = 2 * GLA_QK // GLA_DV
    g_block0 = v_block0 + GLA_HEADS
    row = lambda b, h, t: b * nb + t
    return pl.pallas_call(
        functools.partial(_gla_core_kernel, nchunk=tc // GLA_CHUNK),
        grid=(batch, GLA_HEADS, nb),
        in_specs=[
            pl.BlockSpec((tc, GLA_DK), lambda b, h, t: (row(b, h, t), h)),
            pl.BlockSpec((tc, GLA_DK), lambda b, h, t: (row(b, h, t), qk_blocks + h)),
            pl.BlockSpec((tc, GLA_DV), lambda b, h, t: (row(b, h, t), v_block0 + h)),
            pl.BlockSpec((tc, GLA_DV), lambda b, h, t: (row(b, h, t), g_block0 + h)),
            pl.BlockSpec((tc, GLA_DK), lambda b, h, t: (row(b, h, t), h)),
            pl.BlockSpec((1, GLA_DV), lambda b, h, t: (0, 0)),
        ],
        out_specs=pl.BlockSpec((tc, GLA_DV), lambda b, h, t: (row(b, h, t), h)),
        out_shape=jax.ShapeDtypeStruct((m, GLA_BRANCH), BF16),
        scratch_shapes=[pltpu.VMEM((GLA_DV, GLA_DK), F32)],
        compiler_params=_params("parallel", "parallel", "arbitrary"),
        name="gla_core",
    )(proj, proj, proj, proj, logg, norm_g)


def _outproj_kernel(a_ref, w_ref, pg_ref, x_ref, o_ref):
    y = _dot(a_ref[...], w_ref[...])
    o_ref[...] = x_ref[...] + y * _rms_scale(y) * pg_ref[...]


def _outproj(a, w, post_g, x, *, tm=512):
    m, kdim = a.shape
    d = w.shape[1]
    return pl.pallas_call(
        _outproj_kernel,
        grid=(m // tm,),
        in_specs=[
            pl.BlockSpec((tm, kdim), lambda i: (i, 0)),
            pl.BlockSpec((kdim, d), lambda i: (0, 0)),
            pl.BlockSpec((1, d), lambda i: (0, 0)),
            pl.BlockSpec((tm, d), lambda i: (i, 0)),
        ],
        out_specs=pl.BlockSpec((tm, d), lambda i: (i, 0)),
        out_shape=jax.ShapeDtypeStruct((m, d), F32),
        compiler_params=_params("parallel"),
        name="outproj",
    )(a, w, post_g, x)


def _diff_inproj_kernel(x_ref, g_ref, w_ref, cos_ref, sin_ref, o_ref, *, tn):
    x = x_ref[...]
    h = (x * _rms_scale(x) * g_ref[...]).astype(BF16)
    cos_k, sin_k = cos_ref[...], sin_ref[...]
    cos_q, sin_q = cos_k * DIFF_Q_SCALE, sin_k * DIFF_Q_SCALE
    n_q = DIFF_BRANCH // tn
    for c in range(o_ref.shape[1] // tn):
        y = _dot(h, w_ref[:, c * tn:(c + 1) * tn])
        if c >= 2 * n_q:
            o_ref[:, c * tn:(c + 1) * tn] = y.astype(o_ref.dtype)
            continue
        cos, sin = (cos_q, sin_q) if c < n_q else (cos_k, sin_k)
        for s in range(tn // LANES):
            ysl = y[:, s * LANES:(s + 1) * LANES]
            rot = ysl * cos + pltpu.roll(ysl, LANES // 2, 1) * sin
            o_ref[:, c * tn + s * LANES:c * tn + (s + 1) * LANES] = rot.astype(o_ref.dtype)


def _diff_inproj(x, g, w, cos, sin, *, seq, tm=512, tn=512):
    m, d = x.shape
    n = w.shape[1]
    tblocks = seq // tm
    return pl.pallas_call(
        functools.partial(_diff_inproj_kernel, tn=tn),
        grid=(m // tm,),
        in_specs=[
            pl.BlockSpec((tm, d), lambda i: (i, 0)),
            _resident((1, d)),
            _resident((d, n)),
            pl.BlockSpec((tm, LANES), lambda i: (i % tblocks, 0)),
            pl.BlockSpec((tm, LANES), lambda i: (i % tblocks, 0)),
        ],
        out_specs=pl.BlockSpec((tm, n), lambda i: (i, 0)),
        out_shape=jax.ShapeDtypeStruct((m, n), BF16),
        compiler_params=_params("parallel"),
        name="diff_inproj",
    )(x, g, w, cos, sin)


def _rotary_lane_order(w_cols):
    d_in = w_cols.shape[0]
    half = DIFF_HEAD_DIM // 2
    w5 = w_cols.reshape(d_in, DIFF_HEADS, 2, 2, half)
    return w5.transpose(0, 1, 3, 2, 4).reshape(d_in, DIFF_BRANCH)


def _rope_tables(seq):
    d = DIFF_HEAD_DIM
    inv_freq = 1.0 / (ROPE_THETA ** (jnp.arange(0, d, 2, dtype=F32) / d))
    ang = jnp.arange(seq, dtype=F32)[:, None] * inv_freq[None, :]
    cos, sin = jnp.cos(ang), jnp.sin(ang)
    cos_l = jnp.concatenate([cos, cos, cos, cos], axis=-1)
    sin_l = jnp.concatenate([-sin, -sin, sin, sin], axis=-1)
    return cos_l, sin_l


def _diff_attn_kernel(lq1_ref, lk1_ref, lq2_ref, lk2_ref, q_ref, k_ref, v_ref, gate_ref, ng_ref, o_ref,
                      vt_scr, qs_scr, s_scr, p_scr, m_scr, l_scr, alpha_scr, acc_scr, *, tq, seq, lambda_init):
    qi = pl.program_id(2)
    d = DIFF_HEAD_DIM
    tk = DIFF_KBLOCK
    unroll = tq // tk
    maps = (slice(0, tq), slice(tq, 2 * tq))

    @pl.when(qi == 0)
    def _():
        for r in range(seq // tk):
            vt_scr[r] = v_ref[r * tk:(r + 1) * tk, :].astype(F32).T.astype(BF16)

    qt = q_ref[...].astype(F32).T
    row = lax.broadcasted_iota(jnp.int32, qt.shape, 0)
    is_a = (row // (d // 2)) % 2 == 0
    qs_scr[:, maps[0]] = jnp.where(is_a, qt, 0.0).astype(BF16)
    qs_scr[:, maps[1]] = jnp.where(is_a, 0.0, qt).astype(BF16)
    m_scr[...] = jnp.full_like(m_scr, -jnp.inf)
    l_scr[...] = jnp.zeros_like(l_scr)
    acc_scr[...] = jnp.zeros_like(acc_scr)
    p_scr[...] = jnp.zeros_like(p_scr)
    alpha_scr[...] = jnp.ones_like(alpha_scr)

    def scores(kstart, cs):
        return _dot(k_ref[pl.ds(kstart, tk), :], qs_scr[:, cs])

    def causal(s):
        kpos = lax.broadcasted_iota(jnp.int32, s.shape, 0)
        qpos = lax.broadcasted_iota(jnp.int32, s.shape, 1)
        return jnp.where(kpos <= qpos, s, -jnp.inf)

    def softmax_step(s, cs):
        m_old = m_scr[:, cs]
        m_new = jnp.maximum(m_old, jnp.max(s, axis=0, keepdims=True))
        alpha = jnp.exp2(m_old - m_new)
        p = jnp.exp2(s - m_new)
        l_scr[:, cs] = alpha * l_scr[:, cs] + jnp.sum(p, axis=0, keepdims=True)
        m_scr[:, cs] = m_new
        alpha_scr[:, cs] = alpha
        return p.astype(BF16)

    def value_step(p, vt, cs):
        acc_scr[:, cs] = alpha_scr[:, cs] * acc_scr[:, cs] + _dot(vt, p)

    for mp in range(2):
        s_scr[mp] = scores(0, maps[mp])

    def full_body(jj, carry):
        for u in range(unroll):
            j = jj * unroll + u
            vt_prev = vt_scr[jnp.maximum(j - 1, 0)]
            knext = pl.multiple_of((j + 1) * tk, tk)
            for mp in range(2):
                value_step(p_scr[mp], vt_prev, maps[mp])
                s_next = scores(knext, maps[mp])
                p_scr[mp] = softmax_step(s_scr[mp], maps[mp])
                s_scr[mp] = s_next
        return carry

    n_full = qi * unroll
    lax.fori_loop(0, qi, full_body, 0)

    d0 = pl.multiple_of(qi * tq, tq)
    cols = [[slice(mp * tq + r * tk, (mp + 1) * tq) for mp in range(2)] for r in range(unroll)]
    s_cur = [s_scr[mp] for mp in range(2)]
    p_prev = [p_scr[mp] for mp in range(2)]
    vt_prev = vt_scr[jnp.maximum(n_full - 1, 0)]
    cols_prev = maps
    for r in range(unroll):
        s_nxt = [None, None]
        for mp in range(2):
            value_step(p_prev[mp], vt_prev, cols_prev[mp])
            if r + 1 < unroll:
                s_nxt[mp] = scores(d0 + (r + 1) * tk, cols[r + 1][mp])
            p_prev[mp] = softmax_step(causal(s_cur[mp]), cols[r][mp])
        s_cur, vt_prev, cols_prev = s_nxt, vt_scr[n_full + r], cols[r]
    for mp in range(2):
        value_step(p_prev[mp], vt_prev, cols_prev[mp])

    lam = (jnp.exp(jnp.sum(lq1_ref[...] * lk1_ref[...])) - jnp.exp(jnp.sum(lq2_ref[...] * lk2_ref[...]))
           + lambda_init)
    on = acc_scr[...] * (1.0 / l_scr[...])
    o = (on[:, maps[0]] - lam * on[:, maps[1]]).T
    o = o * _rms_scale(o) * ng_ref[...] * (1.0 - lambda_init)
    gate = gate_ref[...].astype(F32)
    silu = gate / (1.0 + jnp.exp(-gate))
    o_ref[...] = (o * silu).astype(o_ref.dtype)


def _diff_attn(proj, lq1, lk1, lq2, lk2, norm_g, lambda_init, *, batch, seq, tq=1024):
    m = proj.shape[0]
    nq = seq // tq
    hd = 2 * DIFF_HEAD_DIM
    vec = pl.BlockSpec((1, DIFF_HEAD_DIM), lambda b, h, i: (0, 0))
    return pl.pallas_call(
        functools.partial(_diff_attn_kernel, tq=tq, seq=seq, lambda_init=lambda_init),
        grid=(batch, DIFF_HEADS, nq),
        in_specs=[
            vec, vec, vec, vec,
            pl.BlockSpec((tq, hd), lambda b, h, i: (b * nq + i, h)),
            pl.BlockSpec((seq, hd), lambda b, h, i: (b, DIFF_HEADS + h)),
            pl.BlockSpec((seq, hd), lambda b, h, i: (b, 2 * DIFF_HEADS + h)),
            pl.BlockSpec((tq, hd), lambda b, h, i: (b * nq + i, 3 * DIFF_HEADS + h)),
            pl.BlockSpec((1, hd), lambda b, h, i: (0, 0)),
        ],
        out_specs=pl.BlockSpec((tq, hd), lambda b, h, i: (b * nq + i, h)),
        out_shape=jax.ShapeDtypeStruct((m, DIFF_BRANCH), BF16),
        scratch_shapes=[
            pltpu.VMEM((seq // DIFF_KBLOCK, hd, DIFF_KBLOCK), BF16),
            pltpu.VMEM((hd, 2 * tq), BF16),
            pltpu.VMEM((2, DIFF_KBLOCK, tq), F32),
            pltpu.VMEM((2, DIFF_KBLOCK, tq), BF16),
            pltpu.VMEM((1, 2 * tq), F32),
            pltpu.VMEM((1, 2 * tq), F32),
            pltpu.VMEM((1, 2 * tq), F32),
            pltpu.VMEM((hd, 2 * tq), F32),
        ],
        compiler_params=_params("parallel", "parallel", "arbitrary"),
        name="diff_attn",
    )(lq1, lk1, lq2, lk2, proj, proj, proj, proj, norm_g)


def _gla_layer(x, pre_g, post_g, w_in, w_g2, b_g, norm_g, w_out, *, batch, seq):
    w_in = w_in.astype(BF16)
    w_main = w_in[:, :GLA_MAIN]
    w_low = jnp.pad(w_in[:, GLA_MAIN:], ((0, 0), (0, LANES - GLA_RANK)))
    w_g2p = jnp.pad(w_g2.astype(BF16), ((0, LANES - GLA_RANK), (0, 0)))
    proj, logg = _gla_inproj(x, pre_g[None, :], w_main, w_low, w_g2p, b_g[None, :])
    og = _gla_core(proj, logg, norm_g[None, :], batch=batch, seq=seq)
    return _outproj(og, w_out.astype(BF16), post_g[None, :], x)


def _diff_layer(x, pre_g, post_g, w_in, lq1, lk1, lq2, lk2, norm_g, w_out, lambda_init, *, batch, seq):
    cos, sin = _rope_tables(seq)
    w_in = w_in.astype(BF16)
    w_in = jnp.concatenate([_rotary_lane_order(w_in[:, :DIFF_BRANCH]),
                            _rotary_lane_order(w_in[:, DIFF_BRANCH:2 * DIFF_BRANCH]),
                            w_in[:, 2 * DIFF_BRANCH:]], axis=1)
    proj = _diff_inproj(x, pre_g[None, :], w_in, cos, sin, seq=seq)
    og = _diff_attn(proj, lq1[None, :], lk1[None, :], lq2[None, :], lk2[None, :], norm_g[None, :],
                    lambda_init, batch=batch, seq=seq)
    return _outproj(og, w_out.astype(BF16), post_g[None, :], x)


def kernel(x, pre_g, post_g, gla_w_in, gla_w_g2, gla_b_g, gla_norm_g, gla_w_out, diff_w_in,
           diff_lam_q1, diff_lam_k1, diff_lam_q2, diff_lam_k2, diff_norm_g, diff_w_out):
    batch, seq, d_model = x.shape
    depth = pre_g.shape[0]
    xf = x.reshape(batch * seq, d_model)
    for i in range(depth):
        j = i // 2
        if i % 2 == 0:
            xf = _gla_layer(xf, pre_g[i], post_g[i], gla_w_in[j], gla_w_g2[j], gla_b_g[j], gla_norm_g[j],
                            gla_w_out[j], batch=batch, seq=seq)
        else:
            lambda_init = 0.8 - 0.6 * math.exp(-0.3 * i)
            xf = _diff_layer(xf, pre_g[i], post_g[i], diff_w_in[j], diff_lam_q1[j], diff_lam_k1[j],
                             diff_lam_q2[j], diff_lam_k2[j], diff_norm_g[j], diff_w_out[j], lambda_init,
                             batch=batch, seq=seq)
    return xf.reshape(batch, seq, d_model)
```

```python
import functools
import math

import jax
import jax.numpy as jnp
from jax import lax
from jax.experimental import pallas as pl
from jax.experimental.pallas import tpu as pltpu

F32 = jnp.float32
BF16 = jnp.bfloat16

EPS = 1e-6
LANES = 128

GLA_HEADS = 4
GLA_DK = 128
GLA_DV = 512
GLA_RANK = 16
GLA_GATE_NORM = 16.0
GLA_CHUNK = 64
GLA_QK = GLA_HEADS * GLA_DK
GLA_BRANCH = GLA_HEADS * GLA_DV
GLA_MAIN = 2 * GLA_QK + 2 * GLA_BRANCH

DIFF_HEAD_DIM = 64
DIFF_HEADS = 16
DIFF_BRANCH = DIFF_HEADS * 2 * DIFF_HEAD_DIM
DIFF_KBLOCK = 256
DIFF_ONES_ROWS = 16
DIFF_Q_SCALE = DIFF_HEAD_DIM ** -0.5 * math.log2(math.e)
ROPE_THETA = 10000.0

VMEM_LIMIT = 48 * 1024 * 1024


def _dot(a, b):
    return jnp.dot(a, b, preferred_element_type=F32)


def _dot_nt(a, b):
    return lax.dot_general(a, b, (((1,), (1,)), ((), ())), preferred_element_type=F32)


def _dot_tn(a, b):
    return lax.dot_general(a, b, (((0,), (0,)), ((), ())), preferred_element_type=F32)


def _rms_scale(x):
    return lax.rsqrt(jnp.mean(x * x, axis=-1, keepdims=True) + EPS)


def _params(*sem):
    return pltpu.CompilerParams(dimension_semantics=sem, vmem_limit_bytes=VMEM_LIMIT)


def _gla_inproj_kernel(x_ref, g_ref, w_ref, wlow_ref, wg2_ref, bg_ref, qk_ref, vg_ref, logg_ref):
    x = x_ref[...]
    h = (x * _rms_scale(x) * g_ref[...]).astype(BF16)
    for c in range(2 * GLA_QK // GLA_DV):
        y = _dot(h, w_ref[:, c * GLA_DV:(c + 1) * GLA_DV]).astype(qk_ref.dtype)
        for s in range(GLA_DV // GLA_DK):
            qk_ref[c * (GLA_DV // GLA_DK) + s] = y[:, s * GLA_DK:(s + 1) * GLA_DK]
    for c in range(2 * GLA_HEADS):
        c0 = 2 * GLA_QK + c * GLA_DV
        vg_ref[c] = _dot(h, w_ref[:, c0:c0 + GLA_DV]).astype(vg_ref.dtype)
    low = _dot(h, wlow_ref[...])
    z = _dot(low.astype(BF16), wg2_ref[...]) + bg_ref[...]
    log_sig = jnp.minimum(z, 0.0) - jnp.log(1.0 + jnp.exp(-jnp.abs(z)))
    logg = log_sig / GLA_GATE_NORM
    for hd in range(GLA_HEADS):
        logg_ref[hd] = logg[:, hd * GLA_DK:(hd + 1) * GLA_DK]


def _resident(shape):
    return pl.BlockSpec(shape, lambda i: (0,) * len(shape), pipeline_mode=pl.Buffered(1))


def _gla_inproj(x, g, w_main, w_low, w_g2, b_g, *, tm=512):
    m, d = x.shape
    n = w_main.shape[1]
    return pl.pallas_call(
        _gla_inproj_kernel,
        grid=(m // tm,),
        in_specs=[
            pl.BlockSpec((tm, d), lambda i: (i, 0)),
            _resident((1, d)),
            _resident((d, n)),
            _resident((d, LANES)),
            _resident((LANES, GLA_QK)),
            _resident((1, GLA_QK)),
        ],
        out_specs=[
            pl.BlockSpec((2 * GLA_HEADS, tm, GLA_DK), lambda i: (0, i, 0)),
            pl.BlockSpec((2 * GLA_HEADS, tm, GLA_DV), lambda i: (0, i, 0)),
            pl.BlockSpec((GLA_HEADS, tm, GLA_DK), lambda i: (0, i, 0)),
        ],
        out_shape=[
            jax.ShapeDtypeStruct((2 * GLA_HEADS, m, GLA_DK), BF16),
            jax.ShapeDtypeStruct((2 * GLA_HEADS, m, GLA_DV), BF16),
            jax.ShapeDtypeStruct((GLA_HEADS, m, GLA_DK), F32),
        ],
        compiler_params=_params("parallel"),
        name="gla_inproj",
    )(x, g, w_main, w_low, w_g2, b_g)


def _gla_core_kernel(q_ref, k_ref, v_ref, gate_ref, lg_ref, ng_ref, o_ref, st_scr, *, nchunk):
    @pl.when(pl.program_id(2) == 0)
    def _():
        st_scr[...] = jnp.zeros_like(st_scr)

    c = GLA_CHUNK
    tc = nchunk * c
    chunks = [slice(n * c, (n + 1) * c) for n in range(nchunk)]
    row = lax.broadcasted_iota(jnp.int32, (tc, tc), 0)
    col = lax.broadcasted_iota(jnp.int32, (tc, tc), 1)
    causal = (row >= col) & (row // c == col // c)
    tri = jnp.where(causal, 1.0, 0.0).astype(BF16)

    lg = lg_ref[...]
    lg_hi = lg.astype(BF16)
    rem = lg - lg_hi.astype(F32)
    lg_mid = rem.astype(BF16)
    lg_lo = (rem - lg_mid.astype(F32)).astype(BF16)
    bcum = _dot(tri, lg_hi) + _dot(tri, lg_mid) + _dot(tri, lg_lo)
    blast = [bcum[(n + 1) * c - 1:(n + 1) * c, :] for n in range(nchunk)]
    blast_rows = jnp.concatenate([jnp.broadcast_to(b, (c, GLA_DK)) for b in blast], axis=0)

    q = q_ref[...].astype(F32) * (GLA_DK ** -0.5)
    k = k_ref[...].astype(F32)
    v = v_ref[...]
    q_t = (q * jnp.exp(bcum)).astype(BF16)
    k_t = (k * jnp.exp(-bcum)).astype(BF16)
    k_end = (k * jnp.exp(blast_rows - bcum)).astype(BF16)

    a = jnp.where(causal, _dot_nt(q_t, k_t), 0.0).astype(BF16)
    o_intra = _dot(a, v)
    kv = [_dot_tn(v[sl, :], k_end[sl, :]) for sl in chunks]

    st = st_scr[...]
    o_inter = []
    for n, sl in enumerate(chunks):
        o_inter.append(_dot_nt(q_t[sl, :], st.astype(BF16)))
        st = st * jnp.exp(blast[n]) + kv[n]
    st_scr[...] = st

    o = o_intra + jnp.concatenate(o_inter, axis=0)
    gate = gate_ref[...].astype(F32)
    silu = gate * (0.5 + 0.5 * jnp.tanh(0.5 * gate))
    o_ref[...] = (o * _rms_scale(o) * ng_ref[...] * silu).astype(o_ref.dtype)


def _gla_core(qk, vg, logg, norm_g, *, batch, seq, tc=256):
    m = qk.shape[1]
    nb = seq // tc
    row = lambda b, h, t: b * nb + t
    return pl.pallas_call(
        functools.partial(_gla_core_kernel, nchunk=tc // GLA_CHUNK),
        grid=(batch, GLA_HEADS, nb),
        in_specs=[
            pl.BlockSpec((None, tc, GLA_DK), lambda b, h, t: (h, row(b, h, t), 0)),
            pl.BlockSpec((None, tc, GLA_DK), lambda b, h, t: (GLA_HEADS + h, row(b, h, t), 0)),
            pl.BlockSpec((None, tc, GLA_DV), lambda b, h, t: (h, row(b, h, t), 0)),
            pl.BlockSpec((None, tc, GLA_DV), lambda b, h, t: (GLA_HEADS + h, row(b, h, t), 0)),
            pl.BlockSpec((None, tc, GLA_DK), lambda b, h, t: (h, row(b, h, t), 0)),
            pl.BlockSpec((1, GLA_DV), lambda b, h, t: (0, 0)),
        ],
        out_specs=pl.BlockSpec((tc, GLA_DV), lambda b, h, t: (row(b, h, t), h)),
        out_shape=jax.ShapeDtypeStruct((m, GLA_BRANCH), BF16),
        scratch_shapes=[pltpu.VMEM((GLA_DV, GLA_DK), F32)],
        compiler_params=_params("parallel", "parallel", "arbitrary"),
        name="gla_core",
    )(qk, qk, vg, vg, logg, norm_g)


def _outproj_kernel(a_ref, w_ref, pg_ref, x_ref, o_ref):
    y = _dot(a_ref[...], w_ref[...])
    o_ref[...] = x_ref[...] + y * _rms_scale(y) * pg_ref[...]


def _outproj(a, w, post_g, x, *, tm=512):
    m, kdim = a.shape
    d = w.shape[1]
    return pl.pallas_call(
        _outproj_kernel,
        grid=(m // tm,),
        in_specs=[
            pl.BlockSpec((tm, kdim), lambda i: (i, 0)),
            pl.BlockSpec((kdim, d), lambda i: (0, 0)),
            pl.BlockSpec((1, d), lambda i: (0, 0)),
            pl.BlockSpec((tm, d), lambda i: (i, 0)),
        ],
        out_specs=pl.BlockSpec((tm, d), lambda i: (i, 0)),
        out_shape=jax.ShapeDtypeStruct((m, d), F32),
        compiler_params=_params("parallel"),
        name="outproj",
    )(a, w, post_g, x)


def _diff_inproj_kernel(x_ref, g_ref, w_ref, cos_ref, sin_ref, o_ref, *, tn):
    x = x_ref[...]
    h = (x * _rms_scale(x) * g_ref[...]).astype(BF16)
    cos_k, sin_k = cos_ref[...], sin_ref[...]
    cos_q, sin_q = cos_k * DIFF_Q_SCALE, sin_k * DIFF_Q_SCALE
    n_q = DIFF_BRANCH // tn
    per = tn // LANES
    for c in range(w_ref.shape[1] // tn):
        y = _dot(h, w_ref[:, c * tn:(c + 1) * tn])
        cos, sin = (cos_q, sin_q) if c < n_q else (cos_k, sin_k)
        for s in range(per):
            ysl = y[:, s * LANES:(s + 1) * LANES]
            if c < 2 * n_q:
                ysl = ysl * cos + pltpu.roll(ysl, LANES // 2, 1) * sin
            o_ref[c * per + s] = ysl.astype(o_ref.dtype)


def _diff_inproj(x, g, w, cos, sin, *, seq, tm=512, tn=512):
    m, d = x.shape
    n = w.shape[1]
    tblocks = seq // tm
    return pl.pallas_call(
        functools.partial(_diff_inproj_kernel, tn=tn),
        grid=(m // tm,),
        in_specs=[
            pl.BlockSpec((tm, d), lambda i: (i, 0)),
            _resident((1, d)),
            _resident((d, n)),
            pl.BlockSpec((tm, LANES), lambda i: (i % tblocks, 0)),
            pl.BlockSpec((tm, LANES), lambda i: (i % tblocks, 0)),
        ],
        out_specs=pl.BlockSpec((n // LANES, tm, LANES), lambda i: (0, i, 0)),
        out_shape=jax.ShapeDtypeStruct((n // LANES, m, LANES), BF16),
        compiler_params=_params("parallel"),
        name="diff_inproj",
    )(x, g, w, cos, sin)


def _rotary_lane_order(w_cols):
    d_in = w_cols.shape[0]
    half = DIFF_HEAD_DIM // 2
    w5 = w_cols.reshape(d_in, DIFF_HEADS, 2, 2, half)
    return w5.transpose(0, 1, 3, 2, 4).reshape(d_in, DIFF_BRANCH)


def _rope_tables(seq):
    d = DIFF_HEAD_DIM
    inv_freq = 1.0 / (ROPE_THETA ** (jnp.arange(0, d, 2, dtype=F32) / d))
    ang = jnp.arange(seq, dtype=F32)[:, None] * inv_freq[None, :]
    cos, sin = jnp.cos(ang), jnp.sin(ang)
    cos_l = jnp.concatenate([cos, cos, cos, cos], axis=-1)
    sin_l = jnp.concatenate([-sin, -sin, sin, sin], axis=-1)
    return cos_l, sin_l


def _diff_attn_kernel(lq1_ref, lk1_ref, lq2_ref, lk2_ref, q_ref, k_ref, v_ref, gate_ref, ng_ref, o_ref,
                      vt_scr, qs_scr, s_scr, p_scr, m_scr, alpha_scr, acc_scr, *, tq, seq, lambda_init):
    qi = pl.program_id(2)
    d = DIFF_HEAD_DIM
    tk = DIFF_KBLOCK
    unroll = tq // tk
    maps = (slice(0, tq), slice(tq, 2 * tq))

    @pl.when(qi == 0)
    def _():
        ones_row = lax.broadcasted_iota(jnp.int32, (DIFF_ONES_ROWS, tk), 0) == 0
        for r in range(seq // tk):
            vt_scr[r, 0:2 * d, :] = v_ref[r * tk:(r + 1) * tk, :].astype(F32).T.astype(BF16)
            vt_scr[r, 2 * d:, :] = jnp.where(ones_row, 1.0, 0.0).astype(BF16)

    qt = q_ref[...].astype(F32).T
    row = lax.broadcasted_iota(jnp.int32, qt.shape, 0)
    is_a = (row // (d // 2)) % 2 == 0
    qs_scr[:, maps[0]] = jnp.where(is_a, qt, 0.0).astype(BF16)
    qs_scr[:, maps[1]] = jnp.where(is_a, 0.0, qt).astype(BF16)
    m_scr[...] = jnp.full_like(m_scr, -jnp.inf)
    acc_scr[...] = jnp.zeros_like(acc_scr)
    p_scr[...] = jnp.zeros_like(p_scr)
    alpha_scr[...] = jnp.ones_like(alpha_scr)

    def scores(kstart, cs):
        return _dot(k_ref[pl.ds(kstart, tk), :], qs_scr[:, cs])

    def causal(s):
        kpos = lax.broadcasted_iota(jnp.int32, s.shape, 0)
        qpos = lax.broadcasted_iota(jnp.int32, s.shape, 1)
        return jnp.where(kpos <= qpos, s, -jnp.inf)

    def softmax_step(s, cs):
        m_old = m_scr[:, cs]
        m_new = jnp.maximum(m_old, jnp.max(s, axis=0, keepdims=True))
        alpha = jnp.exp2(m_old - m_new)
        p = jnp.exp2(s - m_new)
        m_scr[:, cs] = m_new
        alpha_scr[:, cs] = alpha
        return p.astype(BF16)

    def value_step(p, vt, cs):
        acc_scr[:, cs] = alpha_scr[:, cs] * acc_scr[:, cs] + _dot(vt, p)

    for mp in range(2):
        s_scr[mp] = scores(0, maps[mp])

    def full_body(jj, carry):
        for u in range(unroll):
            j = jj * unroll + u
            vt_prev = vt_scr[jnp.maximum(j - 1, 0)]
            knext = pl.multiple_of((j + 1) * tk, tk)
            for mp in range(2):
                value_step(p_scr[mp], vt_prev, maps[mp])
                s_next = scores(knext, maps[mp])
                p_scr[mp] = softmax_step(s_scr[mp], maps[mp])
                s_scr[mp] = s_next
        return carry

    n_full = qi * unroll
    lax.fori_loop(0, qi, full_body, 0)

    d0 = pl.multiple_of(qi * tq, tq)
    cols = [[slice(mp * tq + r * tk, (mp + 1) * tq) for mp in range(2)] for r in range(unroll)]
    s_cur = [s_scr[mp] for mp in range(2)]
    p_prev = [p_scr[mp] for mp in range(2)]
    vt_prev = vt_scr[jnp.maximum(n_full - 1, 0)]
    cols_prev = maps
    for r in range(unroll):
        s_nxt = [None, None]
        for mp in range(2):
            value_step(p_prev[mp], vt_prev, cols_prev[mp])
            if r + 1 < unroll:
                s_nxt[mp] = scores(d0 + (r + 1) * tk, cols[r + 1][mp])
            p_prev[mp] = softmax_step(causal(s_cur[mp]), cols[r][mp])
        s_cur, vt_prev, cols_prev = s_nxt, vt_scr[n_full + r], cols[r]
    for mp in range(2):
        value_step(p_prev[mp], vt_prev, cols_prev[mp])

    lam = (jnp.exp(jnp.sum(lq1_ref[...] * lk1_ref[...])) - jnp.exp(jnp.sum(lq2_ref[...] * lk2_ref[...]))
           + lambda_init)
    on = acc_scr[0:2 * d, :] * (1.0 / acc_scr[2 * d:2 * d + 1, :])
    o = (on[:, maps[0]] - lam * on[:, maps[1]]).T
    o = o * _rms_scale(o) * ng_ref[...] * (1.0 - lambda_init)
    gate = gate_ref[...].astype(F32)
    silu = gate * (0.5 + 0.5 * jnp.tanh(0.5 * gate))
    o_ref[...] = (o * silu).astype(o_ref.dtype)


def _diff_attn(proj, lq1, lk1, lq2, lk2, norm_g, lambda_init, *, batch, seq, tq=1024):
    m = proj.shape[1]
    nq = seq // tq
    hd = 2 * DIFF_HEAD_DIM
    vec = pl.BlockSpec((1, DIFF_HEAD_DIM), lambda b, h, i: (0, 0))
    return pl.pallas_call(
        functools.partial(_diff_attn_kernel, tq=tq, seq=seq, lambda_init=lambda_init),
        grid=(batch, DIFF_HEADS, nq),
        in_specs=[
            vec, vec, vec, vec,
            pl.BlockSpec((None, tq, hd), lambda b, h, i: (h, b * nq + i, 0)),
            pl.BlockSpec((None, seq, hd), lambda b, h, i: (DIFF_HEADS + h, b, 0)),
            pl.BlockSpec((None, seq, hd), lambda b, h, i: (2 * DIFF_HEADS + h, b, 0)),
            pl.BlockSpec((None, tq, hd), lambda b, h, i: (3 * DIFF_HEADS + h, b * nq + i, 0)),
            pl.BlockSpec((1, hd), lambda b, h, i: (0, 0)),
        ],
        out_specs=pl.BlockSpec((tq, hd), lambda b, h, i: (b * nq + i, h)),
        out_shape=jax.ShapeDtypeStruct((m, DIFF_BRANCH), BF16),
        scratch_shapes=[
            pltpu.VMEM((seq // DIFF_KBLOCK, hd + DIFF_ONES_ROWS, DIFF_KBLOCK), BF16),
            pltpu.VMEM((hd, 2 * tq), BF16),
            pltpu.VMEM((2, DIFF_KBLOCK, tq), F32),
            pltpu.VMEM((2, DIFF_KBLOCK, tq), BF16),
            pltpu.VMEM((1, 2 * tq), F32),
            pltpu.VMEM((1, 2 * tq), F32),
            pltpu.VMEM((hd + DIFF_ONES_ROWS, 2 * tq), F32),
        ],
        compiler_params=_params("parallel", "parallel", "arbitrary"),
        name="diff_attn",
    )(lq1, lk1, lq2, lk2, proj, proj, proj, proj, norm_g)


def _gla_layer(x, pre_g, post_g, w_in, w_g2, b_g, norm_g, w_out, *, batch, seq):
    w_in = w_in.astype(BF16)
    w_main = w_in[:, :GLA_MAIN]
    w_low = jnp.pad(w_in[:, GLA_MAIN:], ((0, 0), (0, LANES - GLA_RANK)))
    w_g2p = jnp.pad(w_g2.astype(BF16), ((0, LANES - GLA_RANK), (0, 0)))
    qk, vg, logg = _gla_inproj(x, pre_g[None, :], w_main, w_low, w_g2p, b_g[None, :])
    og = _gla_core(qk, vg, logg, norm_g[None, :], batch=batch, seq=seq)
    return _outproj(og, w_out.astype(BF16), post_g[None, :], x)


def _diff_layer(x, pre_g, post_g, w_in, lq1, lk1, lq2, lk2, norm_g, w_out, lambda_init, *, batch, seq):
    cos, sin = _rope_tables(seq)
    w_in = w_in.astype(BF16)
    w_in = jnp.concatenate([_rotary_lane_order(w_in[:, :DIFF_BRANCH]),
                            _rotary_lane_order(w_in[:, DIFF_BRANCH:2 * DIFF_BRANCH]),
                            w_in[:, 2 * DIFF_BRANCH:]], axis=1)
    proj = _diff_inproj(x, pre_g[None, :], w_in, cos, sin, seq=seq)
    og = _diff_attn(proj, lq1[None, :], lk1[None, :], lq2[None, :], lk2[None, :], norm_g[None, :],
                    lambda_init, batch=batch, seq=seq)
    return _outproj(og, w_out.astype(BF16), post_g[None, :], x)


def kernel(x, pre_g, post_g, gla_w_in, gla_w_g2, gla_b_g, gla_norm_g, gla_w_out, diff_w_in,
           diff_lam_q1, diff_lam_k1, diff_lam_q2, diff_lam_k2, diff_norm_g, diff_w_out):
    batch, seq, d_model = x.shape
    depth = pre_g.shape[0]
    xf = x.reshape(batch * seq, d_model)
    for i in range(depth):
        j = i // 2
        if i % 2 == 0:
            xf = _gla_layer(xf, pre_g[i], post_g[i], gla_w_in[j], gla_w_g2[j], gla_b_g[j], gla_norm_g[j],
                            gla_w_out[j], batch=batch, seq=seq)
        else:
            lambda_init = 0.8 - 0.6 * math.exp(-0.3 * i)
            xf = _diff_layer(xf, pre_g[i], post_g[i], diff_w_in[j], diff_lam_q1[j], diff_lam_k1[j],
                             diff_lam_q2[j], diff_lam_k2[j], diff_norm_g[j], diff_w_out[j], lambda_init,
                             batch=batch, seq=seq)
    return xf.reshape(batch, seq, d_model)
```

```python
import functools
import math

import jax
import jax.numpy as jnp
from jax import lax
from jax.experimental import pallas as pl
from jax.experimental.pallas import tpu as pltpu

F32 = jnp.float32
BF16 = jnp.bfloat16

EPS = 1e-6
LANES = 128

GLA_HEADS = 4
GLA_DK = 128
GLA_DV = 512
GLA_RANK = 16
GLA_GATE_NORM = 16.0
GLA_CHUNK = 64
GLA_QK = GLA_HEADS * GLA_DK
GLA_BRANCH = GLA_HEADS * GLA_DV
GLA_MAIN = 2 * GLA_QK + 2 * GLA_BRANCH

DIFF_HEAD_DIM = 64
DIFF_HEADS = 16
DIFF_BRANCH = DIFF_HEADS * 2 * DIFF_HEAD_DIM
DIFF_KBLOCK = 256
DIFF_ONES_ROWS = 16
DIFF_Q_SCALE = DIFF_HEAD_DIM ** -0.5 * math.log2(math.e)
ROPE_THETA = 10000.0

VMEM_LIMIT = 48 * 1024 * 1024


def _dot(a, b):
    return jnp.dot(a, b, preferred_element_type=F32)


def _dot_nt(a, b):
    return lax.dot_general(a, b, (((1,), (1,)), ((), ())), preferred_element_type=F32)


def _dot_tn(a, b):
    return lax.dot_general(a, b, (((0,), (0,)), ((), ())), preferred_element_type=F32)


def _rms_scale(x):
    return lax.rsqrt(jnp.mean(x * x, axis=-1, keepdims=True) + EPS)


def _params(*sem):
    return pltpu.CompilerParams(dimension_semantics=sem, vmem_limit_bytes=VMEM_LIMIT)


def _gla_inproj_kernel(x_ref, g_ref, w_ref, wlow_ref, wg2_ref, bg_ref, qk_ref, vg_ref, logg_ref):
    x = x_ref[...]
    h = (x * _rms_scale(x) * g_ref[...]).astype(BF16)
    for c in range(2 * GLA_QK // GLA_DV):
        y = _dot(h, w_ref[:, c * GLA_DV:(c + 1) * GLA_DV]).astype(qk_ref.dtype)
        for s in range(GLA_DV // GLA_DK):
            qk_ref[c * (GLA_DV // GLA_DK) + s] = y[:, s * GLA_DK:(s + 1) * GLA_DK]
    for c in range(2 * GLA_HEADS):
        c0 = 2 * GLA_QK + c * GLA_DV
        vg_ref[c] = _dot(h, w_ref[:, c0:c0 + GLA_DV]).astype(vg_ref.dtype)
    low = _dot(h, wlow_ref[...])
    z = _dot(low.astype(BF16), wg2_ref[...]) + bg_ref[...]
    log_sig = jnp.minimum(z, 0.0) - jnp.log(1.0 + jnp.exp(-jnp.abs(z)))
    logg = log_sig / GLA_GATE_NORM
    for hd in range(GLA_HEADS):
        logg_ref[hd] = logg[:, hd * GLA_DK:(hd + 1) * GLA_DK]


def _resident(shape):
    return pl.BlockSpec(shape, lambda i: (0,) * len(shape), pipeline_mode=pl.Buffered(1))


def _gla_inproj(x, g, w_main, w_low, w_g2, b_g, *, tm=512):
    m, d = x.shape
    n = w_main.shape[1]
    return pl.pallas_call(
        _gla_inproj_kernel,
        grid=(m // tm,),
        in_specs=[
            pl.BlockSpec((tm, d), lambda i: (i, 0)),
            _resident((1, d)),
            _resident((d, n)),
            _resident((d, LANES)),
            _resident((LANES, GLA_QK)),
            _resident((1, GLA_QK)),
        ],
        out_specs=[
            pl.BlockSpec((2 * GLA_HEADS, tm, GLA_DK), lambda i: (0, i, 0)),
            pl.BlockSpec((2 * GLA_HEADS, tm, GLA_DV), lambda i: (0, i, 0)),
            pl.BlockSpec((GLA_HEADS, tm, GLA_DK), lambda i: (0, i, 0)),
        ],
        out_shape=[
            jax.ShapeDtypeStruct((2 * GLA_HEADS, m, GLA_DK), BF16),
            jax.ShapeDtypeStruct((2 * GLA_HEADS, m, GLA_DV), BF16),
            jax.ShapeDtypeStruct((GLA_HEADS, m, GLA_DK), F32),
        ],
        compiler_params=_params("parallel"),
        name="gla_inproj",
    )(x, g, w_main, w_low, w_g2, b_g)


def _gla_core_kernel(q_ref, k_ref, v_ref, gate_ref, lg_ref, ng_ref, o_ref, st_scr, o_scr, *, tc):
    seq = q_ref.shape[0]
    c = GLA_CHUNK
    nchunk = tc // c
    chunks = [slice(n * c, (n + 1) * c) for n in range(nchunk)]
    row = lax.broadcasted_iota(jnp.int32, (tc, tc), 0)
    col = lax.broadcasted_iota(jnp.int32, (tc, tc), 1)
    causal = (row >= col) & (row // c == col // c)
    tri = jnp.where(causal, 1.0, 0.0).astype(BF16)

    def block_rows(t):
        return pl.ds(pl.multiple_of(t * tc, tc), tc)

    def mix(t):
        rows = block_rows(t)
        lg = lg_ref[rows, :]
        lg_hi = lg.astype(BF16)
        rem = lg - lg_hi.astype(F32)
        lg_mid = rem.astype(BF16)
        lg_lo = (rem - lg_mid.astype(F32)).astype(BF16)
        bcum = _dot(tri, lg_hi) + _dot(tri, lg_mid) + _dot(tri, lg_lo)
        blast = [bcum[(n + 1) * c - 1:(n + 1) * c, :] for n in range(nchunk)]
        blast_rows = jnp.concatenate([jnp.broadcast_to(b, (c, GLA_DK)) for b in blast], axis=0)

        q = q_ref[rows, :].astype(F32) * (GLA_DK ** -0.5)
        k = k_ref[rows, :].astype(F32)
        v = v_ref[rows, :]
        q_t = (q * jnp.exp(bcum)).astype(BF16)
        k_t = (k * jnp.exp(-bcum)).astype(BF16)
        k_end = (k * jnp.exp(blast_rows - bcum)).astype(BF16)

        a = jnp.where(causal, _dot_nt(q_t, k_t), 0.0).astype(BF16)
        o_intra = _dot(a, v)
        kv = [_dot_tn(v[sl, :], k_end[sl, :]) for sl in chunks]

        st = st_scr[...]
        for n, sl in enumerate(chunks):
            o_scr[sl, :] = o_intra[sl, :] + _dot_nt(q_t[sl, :], st.astype(BF16))
            st = st * jnp.exp(blast[n]) + kv[n]
        st_scr[...] = st

    def finish(t):
        rows = block_rows(t)
        o = o_scr[...]
        gate = gate_ref[rows, :].astype(F32)
        silu = gate * (0.5 + 0.5 * jnp.tanh(0.5 * gate))
        o_ref[rows, :] = (o * _rms_scale(o) * ng_ref[...] * silu).astype(o_ref.dtype)

    st_scr[...] = jnp.zeros_like(st_scr)
    mix(0)

    def body(t, carry):
        finish(t - 1)
        mix(t)
        return carry

    nb = seq // tc
    lax.fori_loop(1, nb, body, 0)
    finish(nb - 1)


def _gla_core(qk, vg, logg, norm_g, *, batch, seq, tc=256):
    m = qk.shape[1]
    head_rows = lambda part, width: pl.BlockSpec((None, seq, width), lambda b, h: (part * GLA_HEADS + h, b, 0))
    return pl.pallas_call(
        functools.partial(_gla_core_kernel, tc=tc),
        grid=(batch, GLA_HEADS),
        in_specs=[
            head_rows(0, GLA_DK), head_rows(1, GLA_DK), head_rows(0, GLA_DV), head_rows(1, GLA_DV),
            head_rows(0, GLA_DK),
            pl.BlockSpec((1, GLA_DV), lambda b, h: (0, 0)),
        ],
        out_specs=pl.BlockSpec((seq, GLA_DV), lambda b, h: (b, h)),
        out_shape=jax.ShapeDtypeStruct((m, GLA_BRANCH), BF16),
        scratch_shapes=[pltpu.VMEM((GLA_DV, GLA_DK), F32), pltpu.VMEM((tc, GLA_DV), F32)],
        compiler_params=_params("parallel", "parallel"),
        name="gla_core",
    )(qk, qk, vg, vg, logg, norm_g)


def _outproj_kernel(a_ref, w_ref, pg_ref, x_ref, o_ref):
    y = _dot(a_ref[...], w_ref[...])
    o_ref[...] = x_ref[...] + y * _rms_scale(y) * pg_ref[...]


def _outproj(a, w, post_g, x, *, tm=512):
    m, kdim = a.shape
    d = w.shape[1]
    return pl.pallas_call(
        _outproj_kernel,
        grid=(m // tm,),
        in_specs=[
            pl.BlockSpec((tm, kdim), lambda i: (i, 0)),
            pl.BlockSpec((kdim, d), lambda i: (0, 0)),
            pl.BlockSpec((1, d), lambda i: (0, 0)),
            pl.BlockSpec((tm, d), lambda i: (i, 0)),
        ],
        out_specs=pl.BlockSpec((tm, d), lambda i: (i, 0)),
        out_shape=jax.ShapeDtypeStruct((m, d), F32),
        compiler_params=_params("parallel"),
        name="outproj",
    )(a, w, post_g, x)


def _diff_inproj_kernel(x_ref, g_ref, w_ref, cos_ref, sin_ref, o_ref, *, tn):
    x = x_ref[...]
    h = (x * _rms_scale(x) * g_ref[...]).astype(BF16)
    cos_k, sin_k = cos_ref[...], sin_ref[...]
    cos_q, sin_q = cos_k * DIFF_Q_SCALE, sin_k * DIFF_Q_SCALE
    n_q = DIFF_BRANCH // tn
    per = tn // LANES
    for c in range(w_ref.shape[1] // tn):
        y = _dot(h, w_ref[:, c * tn:(c + 1) * tn])
        cos, sin = (cos_q, sin_q) if c < n_q else (cos_k, sin_k)
        for s in range(per):
            ysl = y[:, s * LANES:(s + 1) * LANES]
            if c < 2 * n_q:
                ysl = ysl * cos + pltpu.roll(ysl, LANES // 2, 1) * sin
            o_ref[c * per + s] = ysl.astype(o_ref.dtype)


def _diff_inproj(x, g, w, cos, sin, *, seq, tm=512, tn=512):
    m, d = x.shape
    n = w.shape[1]
    tblocks = seq // tm
    return pl.pallas_call(
        functools.partial(_diff_inproj_kernel, tn=tn),
        grid=(m // tm,),
        in_specs=[
            pl.BlockSpec((tm, d), lambda i: (i, 0)),
            _resident((1, d)),
            _resident((d, n)),
            pl.BlockSpec((tm, LANES), lambda i: (i % tblocks, 0)),
            pl.BlockSpec((tm, LANES), lambda i: (i % tblocks, 0)),
        ],
        out_specs=pl.BlockSpec((n // LANES, tm, LANES), lambda i: (0, i, 0)),
        out_shape=jax.ShapeDtypeStruct((n // LANES, m, LANES), BF16),
        compiler_params=_params("parallel"),
        name="diff_inproj",
    )(x, g, w, cos, sin)


def _rotary_lane_order(w_cols):
    d_in = w_cols.shape[0]
    half = DIFF_HEAD_DIM // 2
    w5 = w_cols.reshape(d_in, DIFF_HEADS, 2, 2, half)
    return w5.transpose(0, 1, 3, 2, 4).reshape(d_in, DIFF_BRANCH)


def _rope_tables(seq):
    d = DIFF_HEAD_DIM
    inv_freq = 1.0 / (ROPE_THETA ** (jnp.arange(0, d, 2, dtype=F32) / d))
    ang = jnp.arange(seq, dtype=F32)[:, None] * inv_freq[None, :]
    cos, sin = jnp.cos(ang), jnp.sin(ang)
    cos_l = jnp.concatenate([cos, cos, cos, cos], axis=-1)
    sin_l = jnp.concatenate([-sin, -sin, sin, sin], axis=-1)
    return cos_l, sin_l


def _diff_attn_kernel(lq1_ref, lk1_ref, lq2_ref, lk2_ref, q_ref, k_ref, v_ref, gate_ref, ng_ref, o_ref,
                      vt_scr, qs_scr, s_scr, p_scr, m_scr, alpha_scr, acc_scr, *, tq, lambda_init):
    seq = q_ref.shape[0]
    d = DIFF_HEAD_DIM
    tk = DIFF_KBLOCK
    unroll = tq // tk
    maps = (slice(0, tq), slice(tq, 2 * tq))

    ones_row = lax.broadcasted_iota(jnp.int32, (DIFF_ONES_ROWS, tk), 0) == 0
    for r in range(seq // tk):
        vt_scr[r, 0:2 * d, :] = v_ref[r * tk:(r + 1) * tk, :].astype(F32).T.astype(BF16)
        vt_scr[r, 2 * d:, :] = jnp.where(ones_row, 1.0, 0.0).astype(BF16)

    lam = (jnp.exp(jnp.sum(lq1_ref[...] * lk1_ref[...])) - jnp.exp(jnp.sum(lq2_ref[...] * lk2_ref[...]))
           + lambda_init)

    def scores(kstart, cs):
        return _dot(k_ref[pl.ds(kstart, tk), :], qs_scr[:, cs])

    def causal(s):
        kpos = lax.broadcasted_iota(jnp.int32, s.shape, 0)
        qpos = lax.broadcasted_iota(jnp.int32, s.shape, 1)
        return jnp.where(kpos <= qpos, s, -jnp.inf)

    def softmax_step(s, cs):
        m_old = m_scr[:, cs]
        m_new = jnp.maximum(m_old, jnp.max(s, axis=0, keepdims=True))
        alpha = jnp.exp2(m_old - m_new)
        p = jnp.exp2(s - m_new)
        m_scr[:, cs] = m_new
        alpha_scr[:, cs] = alpha
        return p.astype(BF16)

    def value_step(p, vt, cs):
        acc_scr[:, cs] = alpha_scr[:, cs] * acc_scr[:, cs] + _dot(vt, p)

    def full_body(jj, carry):
        for u in range(unroll):
            j = jj * unroll + u
            vt_prev = vt_scr[jnp.maximum(j - 1, 0)]
            knext = pl.multiple_of((j + 1) * tk, tk)
            for mp in range(2):
                value_step(p_scr[mp], vt_prev, maps[mp])
                s_next = scores(knext, maps[mp])
                p_scr[mp] = softmax_step(s_scr[mp], maps[mp])
                s_scr[mp] = s_next
        return carry

    for qi in range(seq // tq):
        rows = slice(qi * tq, (qi + 1) * tq)
        qt = q_ref[rows, :].astype(F32).T
        row = lax.broadcasted_iota(jnp.int32, qt.shape, 0)
        is_a = (row // (d // 2)) % 2 == 0
        qs_scr[:, maps[0]] = jnp.where(is_a, qt, 0.0).astype(BF16)
        qs_scr[:, maps[1]] = jnp.where(is_a, 0.0, qt).astype(BF16)
        m_scr[...] = jnp.full_like(m_scr, -jnp.inf)
        acc_scr[...] = jnp.zeros_like(acc_scr)
        p_scr[...] = jnp.zeros_like(p_scr)
        alpha_scr[...] = jnp.ones_like(alpha_scr)

        for mp in range(2):
            s_scr[mp] = scores(0, maps[mp])
        n_full = qi * unroll
        lax.fori_loop(0, qi, full_body, 0)

        d0 = qi * tq
        cols = [[slice(mp * tq + r * tk, (mp + 1) * tq) for mp in range(2)] for r in range(unroll)]
        s_cur = [s_scr[mp] for mp in range(2)]
        p_prev = [p_scr[mp] for mp in range(2)]
        vt_prev = vt_scr[max(n_full - 1, 0)]
        cols_prev = maps
        for r in range(unroll):
            s_nxt = [None, None]
            for mp in range(2):
                value_step(p_prev[mp], vt_prev, cols_prev[mp])
                if r + 1 < unroll:
                    s_nxt[mp] = scores(d0 + (r + 1) * tk, cols[r + 1][mp])
                p_prev[mp] = softmax_step(causal(s_cur[mp]), cols[r][mp])
            s_cur, vt_prev, cols_prev = s_nxt, vt_scr[n_full + r], cols[r]
        for mp in range(2):
            value_step(p_prev[mp], vt_prev, cols_prev[mp])

        inv_l = 1.0 / acc_scr[2 * d:2 * d + 1, :]
        o_t = (acc_scr[0:2 * d, maps[0]] * inv_l[:, maps[0]]
               - acc_scr[0:2 * d, maps[1]] * (lam * inv_l[:, maps[1]]))
        ms = jnp.sum(o_t * o_t, axis=0, keepdims=True) * (1.0 / (2 * d))
        o = (o_t * (lax.rsqrt(ms + EPS) * (1.0 - lambda_init))).T
        gate = gate_ref[rows, :].astype(F32)
        silu = gate * (0.5 + 0.5 * jnp.tanh(0.5 * gate))
        o_ref[rows, :] = (o * ng_ref[...] * silu).astype(o_ref.dtype)


def _diff_attn(proj, lq1, lk1, lq2, lk2, norm_g, lambda_init, *, batch, seq, tq=1024):
    m = proj.shape[1]
    hd = 2 * DIFF_HEAD_DIM
    vec = pl.BlockSpec((1, DIFF_HEAD_DIM), lambda b, h: (0, 0))
    head_rows = lambda part: pl.BlockSpec((None, seq, hd), lambda b, h: (part * DIFF_HEADS + h, b, 0))
    return pl.pallas_call(
        functools.partial(_diff_attn_kernel, tq=tq, lambda_init=lambda_init),
        grid=(batch, DIFF_HEADS),
        in_specs=[vec, vec, vec, vec, head_rows(0), head_rows(1), head_rows(2), head_rows(3),
                  pl.BlockSpec((1, hd), lambda b, h: (0, 0))],
        out_specs=pl.BlockSpec((seq, hd), lambda b, h: (b, h)),
        out_shape=jax.ShapeDtypeStruct((m, DIFF_BRANCH), BF16),
        scratch_shapes=[
            pltpu.VMEM((seq // DIFF_KBLOCK, hd + DIFF_ONES_ROWS, DIFF_KBLOCK), BF16),
            pltpu.VMEM((hd, 2 * tq), BF16),
            pltpu.VMEM((2, DIFF_KBLOCK, tq), F32),
            pltpu.VMEM((2, DIFF_KBLOCK, tq), BF16),
            pltpu.VMEM((1, 2 * tq), F32),
            pltpu.VMEM((1, 2 * tq), F32),
            pltpu.VMEM((hd + DIFF_ONES_ROWS, 2 * tq), F32),
        ],
        compiler_params=_params("parallel", "parallel"),
        name="diff_attn",
    )(lq1, lk1, lq2, lk2, proj, proj, proj, proj, norm_g)


def _gla_layer(x, pre_g, post_g, w_in, w_g2, b_g, norm_g, w_out, *, batch, seq):
    w_in = w_in.astype(BF16)
    w_main = w_in[:, :GLA_MAIN]
    w_low = jnp.pad(w_in[:, GLA_MAIN:], ((0, 0), (0, LANES - GLA_RANK)))
    w_g2p = jnp.pad(w_g2.astype(BF16), ((0, LANES - GLA_RANK), (0, 0)))
    qk, vg, logg = _gla_inproj(x, pre_g[None, :], w_main, w_low, w_g2p, b_g[None, :])
    og = _gla_core(qk, vg, logg, norm_g[None, :], batch=batch, seq=seq)
    return _outproj(og, w_out.astype(BF16), post_g[None, :], x)


def _diff_layer(x, pre_g, post_g, w_in, lq1, lk1, lq2, lk2, norm_g, w_out, lambda_init, *, batch, seq):
    cos, sin = _rope_tables(seq)
    w_in = w_in.astype(BF16)
    w_in = jnp.concatenate([_rotary_lane_order(w_in[:, :DIFF_BRANCH]),
                            _rotary_lane_order(w_in[:, DIFF_BRANCH:2 * DIFF_BRANCH]),
                            w_in[:, 2 * DIFF_BRANCH:]], axis=1)
    proj = _diff_inproj(x, pre_g[None, :], w_in, cos, sin, seq=seq)
    og = _diff_attn(proj, lq1[None, :], lk1[None, :], lq2[None, :], lk2[None, :], norm_g[None, :],
                    lambda_init, batch=batch, seq=seq)
    return _outproj(og, w_out.astype(BF16), post_g[None, :], x)


def kernel(x, pre_g, post_g, gla_w_in, gla_w_g2, gla_b_g, gla_norm_g, gla_w_out, diff_w_in,
           diff_lam_q1, diff_lam_k1, diff_lam_q2, diff_lam_k2, diff_norm_g, diff_w_out):
    batch, seq, d_model = x.shape
    depth = pre_g.shape[0]
    xf = x.reshape(batch * seq, d_model)
    for i in range(depth):
        j = i // 2
        if i % 2 == 0:
            xf = _gla_layer(xf, pre_g[i], post_g[i], gla_w_in[j], gla_w_g2[j], gla_b_g[j], gla_norm_g[j],
                            gla_w_out[j], batch=batch, seq=seq)
        else:
            lambda_init = 0.8 - 0.6 * math.exp(-0.3 * i)
            xf = _diff_layer(xf, pre_g[i], post_g[i], diff_w_in[j], diff_lam_q1[j], diff_lam_k1[j],
                             diff_lam_q2[j], diff_lam_k2[j], diff_norm_g[j], diff_w_out[j], lambda_init,
                             batch=batch, seq=seq)
    return xf.reshape(batch, seq, d_model)
```

```python
import functools
import math

import jax
import jax.numpy as jnp
from jax import lax
from jax.experimental import pallas as pl
from jax.experimental.pallas import tpu as pltpu

F32 = jnp.float32
BF16 = jnp.bfloat16

EPS = 1e-6
LANES = 128

GLA_HEADS = 4
GLA_DK = 128
GLA_DV = 512
GLA_RANK = 16
GLA_GATE_NORM = 16.0
GLA_CHUNK = 64
GLA_QK = GLA_HEADS * GLA_DK
GLA_BRANCH = GLA_HEADS * GLA_DV
GLA_MAIN = 2 * GLA_QK + 2 * GLA_BRANCH

DIFF_HEAD_DIM = 64
DIFF_HEADS = 16
DIFF_BRANCH = DIFF_HEADS * 2 * DIFF_HEAD_DIM
DIFF_KBLOCK = 256
DIFF_ONES_ROWS = 16
DIFF_Q_SCALE = DIFF_HEAD_DIM ** -0.5 * math.log2(math.e)
ROPE_THETA = 10000.0

VMEM_LIMIT = 48 * 1024 * 1024


def _dot(a, b):
    return jnp.dot(a, b, preferred_element_type=F32)


def _dot_nt(a, b):
    return lax.dot_general(a, b, (((1,), (1,)), ((), ())), preferred_element_type=F32)


def _dot_tn(a, b):
    return lax.dot_general(a, b, (((0,), (0,)), ((), ())), preferred_element_type=F32)


def _rms_scale(x):
    return lax.rsqrt(jnp.mean(x * x, axis=-1, keepdims=True) + EPS)


def _params(*sem):
    return pltpu.CompilerParams(dimension_semantics=sem, vmem_limit_bytes=VMEM_LIMIT)


def _gla_inproj_kernel(x_ref, g_ref, w_ref, wlow_ref, wg2_ref, bg_ref, qk_ref, vg_ref, logg_ref):
    x = x_ref[...]
    h = (x * _rms_scale(x) * g_ref[...]).astype(BF16)
    for c in range(2 * GLA_QK // GLA_DV):
        y = _dot(h, w_ref[:, c * GLA_DV:(c + 1) * GLA_DV]).astype(qk_ref.dtype)
        for s in range(GLA_DV // GLA_DK):
            qk_ref[c * (GLA_DV // GLA_DK) + s] = y[:, s * GLA_DK:(s + 1) * GLA_DK]
    for c in range(2 * GLA_HEADS):
        c0 = 2 * GLA_QK + c * GLA_DV
        vg_ref[c] = _dot(h, w_ref[:, c0:c0 + GLA_DV]).astype(vg_ref.dtype)
    low = _dot(h, wlow_ref[...])
    z = _dot(low.astype(BF16), wg2_ref[...]) + bg_ref[...]
    log_sig = jnp.minimum(z, 0.0) - jnp.log(1.0 + jnp.exp(-jnp.abs(z)))
    logg = log_sig / GLA_GATE_NORM
    for hd in range(GLA_HEADS):
        logg_ref[hd] = logg[:, hd * GLA_DK:(hd + 1) * GLA_DK]


def _resident(shape):
    return pl.BlockSpec(shape, lambda i: (0,) * len(shape), pipeline_mode=pl.Buffered(1))


def _gla_inproj(x, g, w_main, w_low, w_g2, b_g, *, tm=512):
    m, d = x.shape
    n = w_main.shape[1]
    return pl.pallas_call(
        _gla_inproj_kernel,
        grid=(m // tm,),
        in_specs=[
            pl.BlockSpec((tm, d), lambda i: (i, 0)),
            _resident((1, d)),
            _resident((d, n)),
            _resident((d, LANES)),
            _resident((LANES, GLA_QK)),
            _resident((1, GLA_QK)),
        ],
        out_specs=[
            pl.BlockSpec((2 * GLA_HEADS, tm, GLA_DK), lambda i: (0, i, 0)),
            pl.BlockSpec((2 * GLA_HEADS, tm, GLA_DV), lambda i: (0, i, 0)),
            pl.BlockSpec((GLA_HEADS, tm, GLA_DK), lambda i: (0, i, 0)),
        ],
        out_shape=[
            jax.ShapeDtypeStruct((2 * GLA_HEADS, m, GLA_DK), BF16),
            jax.ShapeDtypeStruct((2 * GLA_HEADS, m, GLA_DV), BF16),
            jax.ShapeDtypeStruct((GLA_HEADS, m, GLA_DK), F32),
        ],
        compiler_params=_params("parallel"),
        name="gla_inproj",
    )(x, g, w_main, w_low, w_g2, b_g)


def _gla_core_kernel(q_ref, k_ref, v_ref, lg_ref, o_ref, st_scr, *, tc):
    seq = q_ref.shape[0]
    c = GLA_CHUNK
    nchunk = tc // c
    chunks = [slice(n * c, (n + 1) * c) for n in range(nchunk)]
    row = lax.broadcasted_iota(jnp.int32, (tc, tc), 0)
    col = lax.broadcasted_iota(jnp.int32, (tc, tc), 1)
    causal = (row >= col) & (row // c == col // c)
    tri = jnp.where(causal, 1.0, 0.0).astype(BF16)

    def mix(t, carry):
        rows = pl.ds(pl.multiple_of(t * tc, tc), tc)
        lg = lg_ref[rows, :]
        lg_hi = lg.astype(BF16)
        lg_lo = (lg - lg_hi.astype(F32)).astype(BF16)
        cum = _dot(tri, jnp.concatenate([lg_hi, lg_lo], axis=1))
        bcum = cum[:, :GLA_DK] + cum[:, GLA_DK:]
        blast = [bcum[(n + 1) * c - 1:(n + 1) * c, :] for n in range(nchunk)]
        blast_rows = jnp.concatenate([jnp.broadcast_to(b, (c, GLA_DK)) for b in blast], axis=0)

        q = q_ref[rows, :].astype(F32) * (GLA_DK ** -0.5)
        k = k_ref[rows, :].astype(F32)
        v = v_ref[rows, :]
        q_t = (q * jnp.exp(bcum)).astype(BF16)
        k_t = (k * jnp.exp(-bcum)).astype(BF16)
        k_end = (k * jnp.exp(blast_rows - bcum)).astype(BF16)

        a = jnp.where(causal, _dot_nt(q_t, k_t), 0.0).astype(BF16)
        o_intra = _dot(a, v)
        kv = [_dot_tn(v[sl, :], k_end[sl, :]) for sl in chunks]

        st = st_scr[...]
        for n, sl in enumerate(chunks):
            o = o_intra[sl, :] + _dot_nt(q_t[sl, :], st.astype(BF16))
            o_ref[pl.ds(pl.multiple_of(t * tc + n * c, c), c), :] = o.astype(o_ref.dtype)
            st = st * jnp.exp(blast[n]) + kv[n]
        st_scr[...] = st
        return carry

    st_scr[...] = jnp.zeros_like(st_scr)
    lax.fori_loop(0, seq // tc, mix, 0)


def _gla_core(qk, vg, logg, *, batch, seq, tc=256):
    m = qk.shape[1]
    head_rows = lambda part, width: pl.BlockSpec((None, seq, width), lambda b, h: (part * GLA_HEADS + h, b, 0))
    return pl.pallas_call(
        functools.partial(_gla_core_kernel, tc=tc),
        grid=(batch, GLA_HEADS),
        in_specs=[head_rows(0, GLA_DK), head_rows(1, GLA_DK), head_rows(0, GLA_DV), head_rows(0, GLA_DK)],
        out_specs=pl.BlockSpec((None, seq, GLA_DV), lambda b, h: (h, b, 0)),
        out_shape=jax.ShapeDtypeStruct((GLA_HEADS, m, GLA_DV), BF16),
        scratch_shapes=[pltpu.VMEM((GLA_DV, GLA_DK), F32)],
        compiler_params=_params("parallel", "parallel"),
        name="gla_core",
    )(qk, qk, vg, logg)


def _finish_outproj(y, pg_ref, x_ref, o_ref):
    o_ref[...] = x_ref[...] + y * _rms_scale(y) * pg_ref[...]


def _outproj_kernel(a_ref, w_ref, pg_ref, x_ref, o_ref, *, kc):
    heads, _, width = a_ref.shape
    per = kc // width
    y = None
    for j in range(heads // per):
        a = jnp.concatenate([a_ref[j * per + i] for i in range(per)], axis=1)
        part = _dot(a, w_ref[j * kc:(j + 1) * kc, :])
        y = part if y is None else y + part
    _finish_outproj(y, pg_ref, x_ref, o_ref)


def _gla_outproj_kernel(a_ref, gate_ref, ng_ref, w_ref, pg_ref, x_ref, o_ref):
    heads, _, width = a_ref.shape
    y = None
    for hd in range(heads):
        o = a_ref[hd].astype(F32)
        gate = gate_ref[hd]
        silu = gate * (0.5 + 0.5 * jnp.tanh(0.5 * gate))
        a = (o * _rms_scale(o) * ng_ref[...]).astype(BF16) * silu
        part = _dot(a, w_ref[hd * width:(hd + 1) * width, :])
        y = part if y is None else y + part
    _finish_outproj(y, pg_ref, x_ref, o_ref)


def _outproj(a, w, post_g, x, *, gate=None, norm_g=None, tm=512, kc=256):
    heads, m, width = a.shape
    kdim, d = w.shape
    a_spec = pl.BlockSpec((heads, tm, width), lambda i: (0, i, 0))
    tail_specs = [
        pl.BlockSpec((kdim, d), lambda i: (0, 0)),
        pl.BlockSpec((1, d), lambda i: (0, 0)),
        pl.BlockSpec((tm, d), lambda i: (i, 0)),
    ]
    if gate is None:
        body, in_specs, args = functools.partial(_outproj_kernel, kc=kc), [a_spec], (a,)
    else:
        in_specs = [a_spec, pl.BlockSpec((heads, tm, width), lambda i: (1, i, 0)),
                    pl.BlockSpec((1, width), lambda i: (0, 0))]
        body, args = _gla_outproj_kernel, (a, gate, norm_g)
    return pl.pallas_call(
        body,
        grid=(m // tm,),
        in_specs=in_specs + tail_specs,
        out_specs=pl.BlockSpec((tm, d), lambda i: (i, 0)),
        out_shape=jax.ShapeDtypeStruct((m, d), F32),
        compiler_params=_params("parallel"),
        name="outproj",
    )(*args, w, post_g, x)


def _diff_inproj_kernel(x_ref, g_ref, w_ref, cos_ref, sin_ref, o_ref, *, tn):
    x = x_ref[...]
    h = (x * _rms_scale(x) * g_ref[...]).astype(BF16)
    cos_k, sin_k = cos_ref[...], sin_ref[...]
    cos_q, sin_q = cos_k * DIFF_Q_SCALE, sin_k * DIFF_Q_SCALE
    n_q = DIFF_BRANCH // tn
    per = tn // LANES
    for c in range(w_ref.shape[1] // tn):
        y = _dot(h, w_ref[:, c * tn:(c + 1) * tn])
        cos, sin = (cos_q, sin_q) if c < n_q else (cos_k, sin_k)
        for s in range(per):
            ysl = y[:, s * LANES:(s + 1) * LANES]
            if c < 2 * n_q:
                ysl = ysl * cos + pltpu.roll(ysl, LANES // 2, 1) * sin
            o_ref[c * per + s] = ysl.astype(o_ref.dtype)


def _diff_inproj(x, g, w, cos, sin, *, seq, tm=512, tn=512):
    m, d = x.shape
    n = w.shape[1]
    tblocks = seq // tm
    return pl.pallas_call(
        functools.partial(_diff_inproj_kernel, tn=tn),
        grid=(m // tm,),
        in_specs=[
            pl.BlockSpec((tm, d), lambda i: (i, 0)),
            _resident((1, d)),
            _resident((d, n)),
            pl.BlockSpec((tm, LANES), lambda i: (i % tblocks, 0)),
            pl.BlockSpec((tm, LANES), lambda i: (i % tblocks, 0)),
        ],
        out_specs=pl.BlockSpec((n // LANES, tm, LANES), lambda i: (0, i, 0)),
        out_shape=jax.ShapeDtypeStruct((n // LANES, m, LANES), BF16),
        compiler_params=_params("parallel"),
        name="diff_inproj",
    )(x, g, w, cos, sin)


def _rotary_lane_order(w_cols):
    d_in = w_cols.shape[0]
    half = DIFF_HEAD_DIM // 2
    w5 = w_cols.reshape(d_in, DIFF_HEADS, 2, 2, half)
    return w5.transpose(0, 1, 3, 2, 4).reshape(d_in, DIFF_BRANCH)


def _rope_tables(seq):
    d = DIFF_HEAD_DIM
    inv_freq = 1.0 / (ROPE_THETA ** (jnp.arange(0, d, 2, dtype=F32) / d))
    ang = jnp.arange(seq, dtype=F32)[:, None] * inv_freq[None, :]
    cos, sin = jnp.cos(ang), jnp.sin(ang)
    cos_l = jnp.concatenate([cos, cos, cos, cos], axis=-1)
    sin_l = jnp.concatenate([-sin, -sin, sin, sin], axis=-1)
    return cos_l, sin_l


def _diff_attn_kernel(lq1_ref, lk1_ref, lq2_ref, lk2_ref, q_ref, k_ref, v_ref, gate_ref, ng_ref, o_ref,
                      vt_scr, qs_scr, s_scr, pm_scr, p_scr, m_scr, alpha_scr, acc_scr, *, tq, lambda_init):
    seq = q_ref.shape[0]
    d = DIFF_HEAD_DIM
    tk = DIFF_KBLOCK
    unroll = tq // tk
    maps = (slice(0, tq), slice(tq, 2 * tq))

    ones_row = lax.broadcasted_iota(jnp.int32, (DIFF_ONES_ROWS, tk), 0) == 0
    for r in range(seq // tk):
        vt_scr[r, 0:2 * d, :] = v_ref[r * tk:(r + 1) * tk, :].astype(F32).T.astype(BF16)
        vt_scr[r, 2 * d:, :] = jnp.where(ones_row, 1.0, 0.0).astype(BF16)

    lam = (jnp.exp(jnp.sum(lq1_ref[...] * lk1_ref[...])) - jnp.exp(jnp.sum(lq2_ref[...] * lk2_ref[...]))
           + lambda_init)

    def scores(kstart, cs):
        return _dot(k_ref[pl.ds(kstart, tk), :], qs_scr[:, cs])

    def causal(s):
        kpos = lax.broadcasted_iota(jnp.int32, s.shape, 0)
        qpos = lax.broadcasted_iota(jnp.int32, s.shape, 1)
        return jnp.where(kpos <= qpos, s, -jnp.inf)

    def sublane_max(s):
        return jnp.max(s.reshape(s.shape[0] // 8, 8, s.shape[1]), axis=0)

    def softmax_step(s, cs, part_max=None):
        m_old = m_scr[:, cs]
        m_new = jnp.maximum(m_old, jnp.max(s if part_max is None else part_max, axis=0, keepdims=True))
        alpha = jnp.exp2(m_old - m_new)
        p = jnp.exp2(s - m_new)
        m_scr[:, cs] = m_new
        alpha_scr[:, cs] = alpha
        return p.astype(BF16)

    def value_step(p, vt, cs):
        acc_scr[:, cs] = alpha_scr[:, cs] * acc_scr[:, cs] + _dot(vt, p)

    def full_body(jj, carry):
        for u in range(unroll):
            j = jj * unroll + u
            vt_prev = vt_scr[jnp.maximum(j - 1, 0)]
            knext = pl.multiple_of((j + 1) * tk, tk)
            for mp in range(2):
                value_step(p_scr[mp], vt_prev, maps[mp])
                s_next = scores(knext, maps[mp])
                pm_next = sublane_max(s_next)
                p_scr[mp] = softmax_step(s_scr[mp], maps[mp], pm_scr[mp])
                s_scr[mp] = s_next
                pm_scr[mp] = pm_next
        return carry

    for qi in range(seq // tq):
        rows = slice(qi * tq, (qi + 1) * tq)
        qt = q_ref[rows, :].astype(F32).T
        row = lax.broadcasted_iota(jnp.int32, qt.shape, 0)
        is_a = (row // (d // 2)) % 2 == 0
        qs_scr[:, maps[0]] = jnp.where(is_a, qt, 0.0).astype(BF16)
        qs_scr[:, maps[1]] = jnp.where(is_a, 0.0, qt).astype(BF16)
        m_scr[...] = jnp.full_like(m_scr, -jnp.inf)
        acc_scr[...] = jnp.zeros_like(acc_scr)
        p_scr[...] = jnp.zeros_like(p_scr)
        alpha_scr[...] = jnp.ones_like(alpha_scr)

        for mp in range(2):
            s_first = scores(0, maps[mp])
            s_scr[mp] = s_first
            pm_scr[mp] = sublane_max(s_first)
        n_full = qi * unroll
        lax.fori_loop(0, qi, full_body, 0)

        d0 = qi * tq
        cols = [[slice(mp * tq + r * tk, (mp + 1) * tq) for mp in range(2)] for r in range(unroll)]
        s_cur = [s_scr[mp] for mp in range(2)]
        p_prev = [p_scr[mp] for mp in range(2)]
        vt_prev = vt_scr[max(n_full - 1, 0)]
        cols_prev = maps
        for r in range(unroll):
            s_nxt = [None, None]
            for mp in range(2):
                value_step(p_prev[mp], vt_prev, cols_prev[mp])
                if r + 1 < unroll:
                    s_nxt[mp] = scores(d0 + (r + 1) * tk, cols[r + 1][mp])
                p_prev[mp] = softmax_step(causal(s_cur[mp]), cols[r][mp])
            s_cur, vt_prev, cols_prev = s_nxt, vt_scr[n_full + r], cols[r]
        for mp in range(2):
            value_step(p_prev[mp], vt_prev, cols_prev[mp])

        inv_l = 1.0 / acc_scr[2 * d:2 * d + 1, :]
        o_t = (acc_scr[0:2 * d, maps[0]] * inv_l[:, maps[0]]
               - acc_scr[0:2 * d, maps[1]] * (lam * inv_l[:, maps[1]]))
        ms = jnp.sum(o_t * o_t, axis=0, keepdims=True) * (1.0 / (2 * d))
        o = (o_t * (lax.rsqrt(ms + EPS) * (1.0 - lambda_init))).T
        gate = gate_ref[rows, :]
        silu = gate * (0.5 + 0.5 * jnp.tanh(0.5 * gate))
        o_ref[rows, :] = (o * ng_ref[...]).astype(o_ref.dtype) * silu


def _diff_attn(proj, lq1, lk1, lq2, lk2, norm_g, lambda_init, *, batch, seq, tq=1024):
    m = proj.shape[1]
    hd = 2 * DIFF_HEAD_DIM
    vec = pl.BlockSpec((1, DIFF_HEAD_DIM), lambda b, h: (0, 0))
    head_rows = lambda part: pl.BlockSpec((None, seq, hd), lambda b, h: (part * DIFF_HEADS + h, b, 0))
    return pl.pallas_call(
        functools.partial(_diff_attn_kernel, tq=tq, lambda_init=lambda_init),
        grid=(batch, DIFF_HEADS),
        in_specs=[vec, vec, vec, vec, head_rows(0), head_rows(1), head_rows(2), head_rows(3),
                  pl.BlockSpec((1, hd), lambda b, h: (0, 0))],
        out_specs=pl.BlockSpec((None, seq, hd), lambda b, h: (h, b, 0)),
        out_shape=jax.ShapeDtypeStruct((DIFF_HEADS, m, hd), BF16),
        scratch_shapes=[
            pltpu.VMEM((seq // DIFF_KBLOCK, hd + DIFF_ONES_ROWS, DIFF_KBLOCK), BF16),
            pltpu.VMEM((hd, 2 * tq), BF16),
            pltpu.VMEM((2, DIFF_KBLOCK, tq), F32),
            pltpu.VMEM((2, 8, tq), F32),
            pltpu.VMEM((2, DIFF_KBLOCK, tq), BF16),
            pltpu.VMEM((1, 2 * tq), F32),
            pltpu.VMEM((1, 2 * tq), F32),
            pltpu.VMEM((hd + DIFF_ONES_ROWS, 2 * tq), F32),
        ],
        compiler_params=_params("parallel", "parallel"),
        name="diff_attn",
    )(lq1, lk1, lq2, lk2, proj, proj, proj, proj, norm_g)


def _gla_layer(x, pre_g, post_g, w_in, w_g2, b_g, norm_g, w_out, *, batch, seq):
    w_in = w_in.astype(BF16)
    w_main = w_in[:, :GLA_MAIN]
    w_low = jnp.pad(w_in[:, GLA_MAIN:], ((0, 0), (0, LANES - GLA_RANK)))
    w_g2p = jnp.pad(w_g2.astype(BF16), ((0, LANES - GLA_RANK), (0, 0)))
    qk, vg, logg = _gla_inproj(x, pre_g[None, :], w_main, w_low, w_g2p, b_g[None, :])
    og = _gla_core(qk, vg, logg, batch=batch, seq=seq)
    return _outproj(og, w_out.astype(BF16), post_g[None, :], x, gate=vg, norm_g=norm_g[None, :])


def _diff_layer(x, pre_g, post_g, w_in, lq1, lk1, lq2, lk2, norm_g, w_out, lambda_init, *, batch, seq):
    cos, sin = _rope_tables(seq)
    w_in = w_in.astype(BF16)
    w_in = jnp.concatenate([_rotary_lane_order(w_in[:, :DIFF_BRANCH]),
                            _rotary_lane_order(w_in[:, DIFF_BRANCH:2 * DIFF_BRANCH]),
                            w_in[:, 2 * DIFF_BRANCH:]], axis=1)
    proj = _diff_inproj(x, pre_g[None, :], w_in, cos, sin, seq=seq)
    og = _diff_attn(proj, lq1[None, :], lk1[None, :], lq2[None, :], lk2[None, :], norm_g[None, :],
                    lambda_init, batch=batch, seq=seq)
    return _outproj(og, w_out.astype(BF16), post_g[None, :], x)


def kernel(x, pre_g, post_g, gla_w_in, gla_w_g2, gla_b_g, gla_norm_g, gla_w_out, diff_w_in,
           diff_lam_q1, diff_lam_k1, diff_lam_q2, diff_lam_k2, diff_norm_g, diff_w_out):
    batch, seq, d_model = x.shape
    depth = pre_g.shape[0]
    xf = x.reshape(batch * seq, d_model)
    for i in range(depth):
        j = i // 2
        if i % 2 == 0:
            xf = _gla_layer(xf, pre_g[i], post_g[i], gla_w_in[j], gla_w_g2[j], gla_b_g[j], gla_norm_g[j],
                            gla_w_out[j], batch=batch, seq=seq)
        else:
            lambda_init = 0.8 - 0.6 * math.exp(-0.3 * i)
            xf = _diff_layer(xf, pre_g[i], post_g[i], diff_w_in[j], diff_lam_q1[j], diff_lam_k1[j],
                             diff_lam_q2[j], diff_lam_k2[j], diff_norm_g[j], diff_w_out[j], lambda_init,
                             batch=batch, seq=seq)
    return xf.reshape(batch, seq, d_model)
```

```python
import functools
import math

import jax
import jax.numpy as jnp
from jax import lax
from jax.experimental import pallas as pl
from jax.experimental.pallas import tpu as pltpu

F32 = jnp.float32
BF16 = jnp.bfloat16

EPS = 1e-6
LANES = 128

GLA_HEADS = 4
GLA_DK = 128
GLA_DV = 512
GLA_RANK = 16
GLA_GATE_NORM = 16.0
GLA_CHUNK = 64
GLA_CUM_ROWS = 256
GLA_QK = GLA_HEADS * GLA_DK
GLA_BRANCH = GLA_HEADS * GLA_DV
GLA_MAIN = 2 * GLA_QK + 2 * GLA_BRANCH

DIFF_HEAD_DIM = 64
DIFF_HEADS = 16
DIFF_BRANCH = DIFF_HEADS * 2 * DIFF_HEAD_DIM
DIFF_KBLOCK = 256
DIFF_ONES_ROWS = 16
DIFF_Q_SCALE = DIFF_HEAD_DIM ** -0.5 * math.log2(math.e)
ROPE_THETA = 10000.0

VMEM_LIMIT = 48 * 1024 * 1024


def _dot(a, b):
    return jnp.dot(a, b, preferred_element_type=F32)


def _dot_nt(a, b):
    return lax.dot_general(a, b, (((1,), (1,)), ((), ())), preferred_element_type=F32)


def _dot_tn(a, b):
    return lax.dot_general(a, b, (((0,), (0,)), ((), ())), preferred_element_type=F32)


def _rms_scale(x):
    return lax.rsqrt(jnp.mean(x * x, axis=-1, keepdims=True) + EPS)


def _params(*sem):
    return pltpu.CompilerParams(dimension_semantics=sem, vmem_limit_bytes=VMEM_LIMIT)


def _gla_inproj_kernel(x_ref, g_ref, w_ref, wlow_ref, wg2_ref, bg_ref, qkk_ref, vg_ref, eb_ref):
    tm = x_ref.shape[0]
    c = GLA_CHUNK
    sub = GLA_CUM_ROWS
    x = x_ref[...]
    h = (x * _rms_scale(x) * g_ref[...]).astype(BF16)
    low = _dot(h, wlow_ref[...])
    z = _dot(low.astype(BF16), wg2_ref[...]) + bg_ref[...]
    logg = (jnp.minimum(z, 0.0) - jnp.log(1.0 + jnp.exp(-jnp.abs(z)))) / GLA_GATE_NORM

    row = lax.broadcasted_iota(jnp.int32, (sub, sub), 0)
    col = lax.broadcasted_iota(jnp.int32, (sub, sub), 1)
    tri = jnp.where((row >= col) & (row // c == col // c), 1.0, 0.0).astype(BF16)
    lg_hi = logg.astype(BF16)
    lg_lo = (logg - lg_hi.astype(F32)).astype(BF16)
    split = jnp.concatenate([lg_hi, lg_lo], axis=1)
    cum = jnp.concatenate([_dot(tri, split[r * sub:(r + 1) * sub, :]) for r in range(tm // sub)], axis=0)
    bcum = cum[:, :GLA_QK] + cum[:, GLA_QK:]
    blast = [bcum[(n + 1) * c - 1:(n + 1) * c, :] for n in range(tm // c)]
    blast_rows = jnp.concatenate([jnp.broadcast_to(bl, (c, GLA_QK)) for bl in blast], axis=0)
    eb = jnp.exp(jnp.concatenate(blast, axis=0))

    q = _dot(h, w_ref[:, 0:GLA_QK]) * (GLA_DK ** -0.5)
    k = _dot(h, w_ref[:, GLA_QK:2 * GLA_QK])
    parts = (q * jnp.exp(bcum), k * jnp.exp(-bcum), k * jnp.exp(blast_rows - bcum))
    for i, part in enumerate(parts):
        part = part.astype(qkk_ref.dtype)
        for hd in range(GLA_HEADS):
            qkk_ref[i * GLA_HEADS + hd] = part[:, hd * GLA_DK:(hd + 1) * GLA_DK]
    for hd in range(GLA_HEADS):
        eb_ref[hd] = eb[:, hd * GLA_DK:(hd + 1) * GLA_DK]
    for cc in range(2 * GLA_HEADS):
        c0 = 2 * GLA_QK + cc * GLA_DV
        vg_ref[cc] = _dot(h, w_ref[:, c0:c0 + GLA_DV]).astype(vg_ref.dtype)


def _resident(shape):
    return pl.BlockSpec(shape, lambda i: (0,) * len(shape), pipeline_mode=pl.Buffered(1))


def _gla_inproj(x, g, w_main, w_low, w_g2, b_g, *, tm=512):
    m, d = x.shape
    n = w_main.shape[1]
    return pl.pallas_call(
        _gla_inproj_kernel,
        grid=(m // tm,),
        in_specs=[
            pl.BlockSpec((tm, d), lambda i: (i, 0)),
            _resident((1, d)),
            _resident((d, n)),
            _resident((d, LANES)),
            _resident((LANES, GLA_QK)),
            _resident((1, GLA_QK)),
        ],
        out_specs=[
            pl.BlockSpec((3 * GLA_HEADS, tm, GLA_DK), lambda i: (0, i, 0)),
            pl.BlockSpec((2 * GLA_HEADS, tm, GLA_DV), lambda i: (0, i, 0)),
            pl.BlockSpec((GLA_HEADS, tm // GLA_CHUNK, GLA_DK), lambda i: (0, i, 0)),
        ],
        out_shape=[
            jax.ShapeDtypeStruct((3 * GLA_HEADS, m, GLA_DK), BF16),
            jax.ShapeDtypeStruct((2 * GLA_HEADS, m, GLA_DV), BF16),
            jax.ShapeDtypeStruct((GLA_HEADS, m // GLA_CHUNK, GLA_DK), F32),
        ],
        compiler_params=_params("parallel"),
        name="gla_inproj",
    )(x, g, w_main, w_low, w_g2, b_g)


def _gla_core_kernel(qt_ref, kt_ref, ke_ref, v_ref, eb_ref, o_ref, st_scr, oi_scr, kv_scr, *, tc):
    seq = qt_ref.shape[0]
    c = GLA_CHUNK
    nchunk = tc // c
    nb = seq // tc
    chunks = [slice(n * c, (n + 1) * c) for n in range(nchunk)]
    row = lax.broadcasted_iota(jnp.int32, (tc, tc), 0)
    col = lax.broadcasted_iota(jnp.int32, (tc, tc), 1)
    causal = (row >= col) & (row // c == col // c)
    own_chunk = (lax.broadcasted_iota(jnp.int32, (tc, nchunk * GLA_DK), 0) // c
                 == lax.broadcasted_iota(jnp.int32, (tc, nchunk * GLA_DK), 1) // GLA_DK)

    def block_rows(t):
        return pl.ds(pl.multiple_of(t * tc, tc), tc)

    def products(t):
        slot = t % 2
        rows = block_rows(t)
        v = v_ref[rows, :]
        a = jnp.where(causal, _dot_nt(qt_ref[rows, :], kt_ref[rows, :]), 0.0).astype(BF16)
        oi_scr[slot] = _dot(a, v)
        ke = jnp.concatenate([ke_ref[rows, :]] * nchunk, axis=1)
        ke = jnp.where(own_chunk, ke, jnp.zeros_like(ke))
        kv_scr[slot] = _dot_tn(v, ke)

    def recur(t):
        slot = t % 2
        st = st_scr[...]
        for n, sl in enumerate(chunks):
            rows = pl.ds(pl.multiple_of(t * tc + n * c, c), c)
            o = oi_scr[slot, sl, :] + _dot_nt(qt_ref[rows, :], st.astype(BF16))
            o_ref[rows, :] = o.astype(o_ref.dtype)
            st = st * eb_ref[pl.ds(t * nchunk + n, 1), :] + kv_scr[slot, :, n * GLA_DK:(n + 1) * GLA_DK]
        st_scr[...] = st

    st_scr[...] = jnp.zeros_like(st_scr)
    products(0)

    def body(t, carry):
        recur(t)
        products(t + 1)
        return carry

    lax.fori_loop(0, nb - 1, body, 0)
    recur(nb - 1)


def _gla_core(qkk, vg, eb, *, batch, seq, tc=256):
    m = qkk.shape[1]
    nchunk = tc // GLA_CHUNK
    head_rows = lambda part, width: pl.BlockSpec((None, seq, width), lambda b, h: (part * GLA_HEADS + h, b, 0))
    return pl.pallas_call(
        functools.partial(_gla_core_kernel, tc=tc),
        grid=(batch, GLA_HEADS),
        in_specs=[head_rows(0, GLA_DK), head_rows(1, GLA_DK), head_rows(2, GLA_DK), head_rows(0, GLA_DV),
                  pl.BlockSpec((None, seq // GLA_CHUNK, GLA_DK), lambda b, h: (h, b, 0))],
        out_specs=pl.BlockSpec((None, seq, GLA_DV), lambda b, h: (h, b, 0)),
        out_shape=jax.ShapeDtypeStruct((GLA_HEADS, m, GLA_DV), BF16),
        scratch_shapes=[
            pltpu.VMEM((GLA_DV, GLA_DK), F32),
            pltpu.VMEM((2, tc, GLA_DV), F32),
            pltpu.VMEM((2, GLA_DV, nchunk * GLA_DK), F32),
        ],
        compiler_params=_params("parallel", "parallel"),
        name="gla_core",
    )(qkk, qkk, qkk, vg, eb)


def _finish_outproj(y, pg_ref, x_ref, o_ref):
    o_ref[...] = x_ref[...] + y * _rms_scale(y) * pg_ref[...]


def _outproj_kernel(a_ref, w_ref, pg_ref, x_ref, o_ref, *, kc):
    heads, _, width = a_ref.shape
    per = kc // width
    y = None
    for j in range(heads // per):
        a = jnp.concatenate([a_ref[j * per + i] for i in range(per)], axis=1)
        part = _dot(a, w_ref[j * kc:(j + 1) * kc, :])
        y = part if y is None else y + part
    _finish_outproj(y, pg_ref, x_ref, o_ref)


def _gla_outproj_kernel(a_ref, gate_ref, ng_ref, w_ref, pg_ref, x_ref, o_ref):
    heads, _, width = a_ref.shape
    y = None
    for hd in range(heads):
        o = a_ref[hd].astype(F32)
        gate = gate_ref[hd]
        silu = gate * (0.5 + 0.5 * jnp.tanh(0.5 * gate))
        a = (o * _rms_scale(o) * ng_ref[...]).astype(BF16) * silu
        part = _dot(a, w_ref[hd * width:(hd + 1) * width, :])
        y = part if y is None else y + part
    _finish_outproj(y, pg_ref, x_ref, o_ref)


def _outproj(a, w, post_g, x, *, gate=None, norm_g=None, tm=512, kc=256):
    heads, m, width = a.shape
    kdim, d = w.shape
    a_spec = pl.BlockSpec((heads, tm, width), lambda i: (0, i, 0))
    tail_specs = [
        pl.BlockSpec((kdim, d), lambda i: (0, 0)),
        pl.BlockSpec((1, d), lambda i: (0, 0)),
        pl.BlockSpec((tm, d), lambda i: (i, 0)),
    ]
    if gate is None:
        body, in_specs, args = functools.partial(_outproj_kernel, kc=kc), [a_spec], (a,)
    else:
        in_specs = [a_spec, pl.BlockSpec((heads, tm, width), lambda i: (1, i, 0)),
                    pl.BlockSpec((1, width), lambda i: (0, 0))]
        body, args = _gla_outproj_kernel, (a, gate, norm_g)
    return pl.pallas_call(
        body,
        grid=(m // tm,),
        in_specs=in_specs + tail_specs,
        out_specs=pl.BlockSpec((tm, d), lambda i: (i, 0)),
        out_shape=jax.ShapeDtypeStruct((m, d), F32),
        compiler_params=_params("parallel"),
        name="outproj",
    )(*args, w, post_g, x)


def _diff_inproj_kernel(x_ref, g_ref, w_ref, cos_ref, sin_ref, o_ref, *, tn):
    x = x_ref[...]
    h = (x * _rms_scale(x) * g_ref[...]).astype(BF16)
    cos_k, sin_k = cos_ref[...], sin_ref[...]
    cos_q, sin_q = cos_k * DIFF_Q_SCALE, sin_k * DIFF_Q_SCALE
    n_q = DIFF_BRANCH // tn
    per = tn // LANES
    for c in range(w_ref.shape[1] // tn):
        y = _dot(h, w_ref[:, c * tn:(c + 1) * tn])
        cos, sin = (cos_q, sin_q) if c < n_q else (cos_k, sin_k)
        for s in range(per):
            ysl = y[:, s * LANES:(s + 1) * LANES]
            if c < 2 * n_q:
                ysl = ysl * cos + pltpu.roll(ysl, LANES // 2, 1) * sin
            o_ref[c * per + s] = ysl.astype(o_ref.dtype)


def _diff_inproj(x, g, w, cos, sin, *, seq, tm=512, tn=512):
    m, d = x.shape
    n = w.shape[1]
    tblocks = seq // tm
    return pl.pallas_call(
        functools.partial(_diff_inproj_kernel, tn=tn),
        grid=(m // tm,),
        in_specs=[
            pl.BlockSpec((tm, d), lambda i: (i, 0)),
            _resident((1, d)),
            _resident((d, n)),
            pl.BlockSpec((tm, LANES), lambda i: (i % tblocks, 0)),
            pl.BlockSpec((tm, LANES), lambda i: (i % tblocks, 0)),
        ],
        out_specs=pl.BlockSpec((n // LANES, tm, LANES), lambda i: (0, i, 0)),
        out_shape=jax.ShapeDtypeStruct((n // LANES, m, LANES), BF16),
        compiler_params=_params("parallel"),
        name="diff_inproj",
    )(x, g, w, cos, sin)


def _rotary_lane_order(w_cols):
    d_in = w_cols.shape[0]
    half = DIFF_HEAD_DIM // 2
    w5 = w_cols.reshape(d_in, DIFF_HEADS, 2, 2, half)
    return w5.transpose(0, 1, 3, 2, 4).reshape(d_in, DIFF_BRANCH)


def _rope_tables(seq):
    d = DIFF_HEAD_DIM
    inv_freq = 1.0 / (ROPE_THETA ** (jnp.arange(0, d, 2, dtype=F32) / d))
    ang = jnp.arange(seq, dtype=F32)[:, None] * inv_freq[None, :]
    cos, sin = jnp.cos(ang), jnp.sin(ang)
    cos_l = jnp.concatenate([cos, cos, cos, cos], axis=-1)
    sin_l = jnp.concatenate([-sin, -sin, sin, sin], axis=-1)
    return cos_l, sin_l


def _diff_attn_kernel(lq1_ref, lk1_ref, lq2_ref, lk2_ref, q_ref, k_ref, v_ref, gate_ref, ng_ref, o_ref,
                      vt_scr, qs_scr, s_scr, pm_scr, p_scr, m_scr, alpha_scr, acc_scr, *, tq, lambda_init):
    seq = q_ref.shape[0]
    d = DIFF_HEAD_DIM
    tk = DIFF_KBLOCK
    unroll = tq // tk
    maps = (slice(0, tq), slice(tq, 2 * tq))

    ones_row = lax.broadcasted_iota(jnp.int32, (DIFF_ONES_ROWS, tk), 0) == 0
    for r in range(seq // tk):
        vt_scr[r, 0:2 * d, :] = v_ref[r * tk:(r + 1) * tk, :].astype(F32).T.astype(BF16)
        vt_scr[r, 2 * d:, :] = jnp.where(ones_row, 1.0, 0.0).astype(BF16)

    lam = (jnp.exp(jnp.sum(lq1_ref[...] * lk1_ref[...])) - jnp.exp(jnp.sum(lq2_ref[...] * lk2_ref[...]))
           + lambda_init)

    def scores(kstart, cs):
        return _dot(k_ref[pl.ds(kstart, tk), :], qs_scr[:, cs])

    def causal(s):
        kpos = lax.broadcasted_iota(jnp.int32, s.shape, 0)
        qpos = lax.broadcasted_iota(jnp.int32, s.shape, 1)
        return jnp.where(kpos <= qpos, s, -jnp.inf)

    def sublane_max(s):
        return jnp.max(s.reshape(s.shape[0] // 8, 8, s.shape[1]), axis=0)

    def softmax_step(s, cs, part_max=None):
        m_old = m_scr[:, cs]
        m_new = jnp.maximum(m_old, jnp.max(s if part_max is None else part_max, axis=0, keepdims=True))
        alpha = jnp.exp2(m_old - m_new)
        p = jnp.exp2(s - m_new)
        m_scr[:, cs] = m_new
        alpha_scr[:, cs] = alpha
        return p.astype(BF16)

    def value_step(p, vt, cs):
        acc_scr[:, cs] = alpha_scr[:, cs] * acc_scr[:, cs] + _dot(vt, p)

    def full_body(jj, carry):
        for u in range(unroll):
            j = jj * unroll + u
            vt_prev = vt_scr[jnp.maximum(j - 1, 0)]
            knext = pl.multiple_of((j + 1) * tk, tk)
            for mp in range(2):
                value_step(p_scr[mp], vt_prev, maps[mp])
                s_next = scores(knext, maps[mp])
                pm_next = sublane_max(s_next)
                p_scr[mp] = softmax_step(s_scr[mp], maps[mp], pm_scr[mp])
                s_scr[mp] = s_next
                pm_scr[mp] = pm_next
        return carry

    for qi in range(seq // tq):
        rows = slice(qi * tq, (qi + 1) * tq)
        qt = q_ref[rows, :].astype(F32).T
        row = lax.broadcasted_iota(jnp.int32, qt.shape, 0)
        is_a = (row // (d // 2)) % 2 == 0
        qs_scr[:, maps[0]] = jnp.where(is_a, qt, 0.0).astype(BF16)
        qs_scr[:, maps[1]] = jnp.where(is_a, 0.0, qt).astype(BF16)
        m_scr[...] = jnp.full_like(m_scr, -jnp.inf)
        acc_scr[...] = jnp.zeros_like(acc_scr)
        p_scr[...] = jnp.zeros_like(p_scr)
        alpha_scr[...] = jnp.ones_like(alpha_scr)

        for mp in range(2):
            s_first = scores(0, maps[mp])
            s_scr[mp] = s_first
            pm_scr[mp] = sublane_max(s_first)
        n_full = qi * unroll
        lax.fori_loop(0, qi, full_body, 0)

        d0 = qi * tq
        cols = [[slice(mp * tq + r * tk, (mp + 1) * tq) for mp in range(2)] for r in range(unroll)]
        s_cur = [s_scr[mp] for mp in range(2)]
        p_prev = [p_scr[mp] for mp in range(2)]
        vt_prev = vt_scr[max(n_full - 1, 0)]
        cols_prev = maps
        for r in range(unroll):
            s_nxt = [None, None]
            for mp in range(2):
                value_step(p_prev[mp], vt_prev, cols_prev[mp])
                if r + 1 < unroll:
                    s_nxt[mp] = scores(d0 + (r + 1) * tk, cols[r + 1][mp])
                p_prev[mp] = softmax_step(causal(s_cur[mp]), cols[r][mp])
            s_cur, vt_prev, cols_prev = s_nxt, vt_scr[n_full + r], cols[r]
        for mp in range(2):
            value_step(p_prev[mp], vt_prev, cols_prev[mp])

        inv_l = 1.0 / acc_scr[2 * d:2 * d + 1, :]
        o_t = (acc_scr[0:2 * d, maps[0]] * inv_l[:, maps[0]]
               - acc_scr[0:2 * d, maps[1]] * (lam * inv_l[:, maps[1]]))
        ms = jnp.sum(o_t * o_t, axis=0, keepdims=True) * (1.0 / (2 * d))
        o = (o_t * (lax.rsqrt(ms + EPS) * (1.0 - lambda_init))).T
        gate = gate_ref[rows, :]
        silu = gate * (0.5 + 0.5 * jnp.tanh(0.5 * gate))
        o_ref[rows, :] = (o * ng_ref[...]).astype(o_ref.dtype) * silu


def _diff_attn(proj, lq1, lk1, lq2, lk2, norm_g, lambda_init, *, batch, seq, tq=1024):
    m = proj.shape[1]
    hd = 2 * DIFF_HEAD_DIM
    vec = pl.BlockSpec((1, DIFF_HEAD_DIM), lambda b, h: (0, 0))
    head_rows = lambda part: pl.BlockSpec((None, seq, hd), lambda b, h: (part * DIFF_HEADS + h, b, 0))
    return pl.pallas_call(
        functools.partial(_diff_attn_kernel, tq=tq, lambda_init=lambda_init),
        grid=(batch, DIFF_HEADS),
        in_specs=[vec, vec, vec, vec, head_rows(0), head_rows(1), head_rows(2), head_rows(3),
                  pl.BlockSpec((1, hd), lambda b, h: (0, 0))],
        out_specs=pl.BlockSpec((None, seq, hd), lambda b, h: (h, b, 0)),
        out_shape=jax.ShapeDtypeStruct((DIFF_HEADS, m, hd), BF16),
        scratch_shapes=[
            pltpu.VMEM((seq // DIFF_KBLOCK, hd + DIFF_ONES_ROWS, DIFF_KBLOCK), BF16),
            pltpu.VMEM((hd, 2 * tq), BF16),
            pltpu.VMEM((2, DIFF_KBLOCK, tq), F32),
            pltpu.VMEM((2, 8, tq), F32),
            pltpu.VMEM((2, DIFF_KBLOCK, tq), BF16),
            pltpu.VMEM((1, 2 * tq), F32),
            pltpu.VMEM((1, 2 * tq), F32),
            pltpu.VMEM((hd + DIFF_ONES_ROWS, 2 * tq), F32),
        ],
        compiler_params=_params("parallel", "parallel"),
        name="diff_attn",
    )(lq1, lk1, lq2, lk2, proj, proj, proj, proj, norm_g)


def _gla_layer(x, pre_g, post_g, w_in, w_g2, b_g, norm_g, w_out, *, batch, seq):
    w_in = w_in.astype(BF16)
    w_main = w_in[:, :GLA_MAIN]
    w_low = jnp.pad(w_in[:, GLA_MAIN:], ((0, 0), (0, LANES - GLA_RANK)))
    w_g2p = jnp.pad(w_g2.astype(BF16), ((0, LANES - GLA_RANK), (0, 0)))
    qkk, vg, eb = _gla_inproj(x, pre_g[None, :], w_main, w_low, w_g2p, b_g[None, :])
    og = _gla_core(qkk, vg, eb, batch=batch, seq=seq)
    return _outproj(og, w_out.astype(BF16), post_g[None, :], x, gate=vg, norm_g=norm_g[None, :])


def _diff_layer(x, pre_g, post_g, w_in, lq1, lk1, lq2, lk2, norm_g, w_out, lambda_init, *, batch, seq):
    cos, sin = _rope_tables(seq)
    w_in = w_in.astype(BF16)
    w_in = jnp.concatenate([_rotary_lane_order(w_in[:, :DIFF_BRANCH]),
                            _rotary_lane_order(w_in[:, DIFF_BRANCH:2 * DIFF_BRANCH]),
                            w_in[:, 2 * DIFF_BRANCH:]], axis=1)
    proj = _diff_inproj(x, pre_g[None, :], w_in, cos, sin, seq=seq)
    og = _diff_attn(proj, lq1[None, :], lk1[None, :], lq2[None, :], lk2[None, :], norm_g[None, :],
                    lambda_init, batch=batch, seq=seq)
    return _outproj(og, w_out.astype(BF16), post_g[None, :], x)


def kernel(x, pre_g, post_g, gla_w_in, gla_w_g2, gla_b_g, gla_norm_g, gla_w_out, diff_w_in,
           diff_lam_q1, diff_lam_k1, diff_lam_q2, diff_lam_k2, diff_norm_g, diff_w_out):
    batch, seq, d_model = x.shape
    depth = pre_g.shape[0]
    xf = x.reshape(batch * seq, d_model)
    for i in range(depth):
        j = i // 2
        if i % 2 == 0:
            xf = _gla_layer(xf, pre_g[i], post_g[i], gla_w_in[j], gla_w_g2[j], gla_b_g[j], gla_norm_g[j],
                            gla_w_out[j], batch=batch, seq=seq)
        else:
            lambda_init = 0.8 - 0.6 * math.exp(-0.3 * i)
            xf = _diff_layer(xf, pre_g[i], post_g[i], diff_w_in[j], diff_lam_q1[j], diff_lam_k1[j],
                             diff_lam_q2[j], diff_lam_k2[j], diff_norm_g[j], diff_w_out[j], lambda_init,
                             batch=batch, seq=seq)
    return xf.reshape(batch, seq, d_model)
```

```python
import functools
import math

import jax
import jax.numpy as jnp
from jax import lax
from jax.experimental import pallas as pl
from jax.experimental.pallas import tpu as pltpu

F32 = jnp.float32
BF16 = jnp.bfloat16

EPS = 1e-6
LANES = 128

GLA_HEADS = 4
GLA_DK = 128
GLA_DV = 512
GLA_RANK = 16
GLA_GATE_NORM = 16.0
GLA_CHUNK = 64
GLA_CUM_ROWS = 256
GLA_QK = GLA_HEADS * GLA_DK
GLA_BRANCH = GLA_HEADS * GLA_DV
GLA_MAIN = 2 * GLA_QK + 2 * GLA_BRANCH

DIFF_HEAD_DIM = 64
DIFF_HEADS = 16
DIFF_BRANCH = DIFF_HEADS * 2 * DIFF_HEAD_DIM
DIFF_KBLOCK = 256
DIFF_ONES_ROWS = 16
DIFF_Q_SCALE = DIFF_HEAD_DIM ** -0.5 * math.log2(math.e)
ROPE_THETA = 10000.0

VMEM_LIMIT = 48 * 1024 * 1024


def _dot(a, b):
    return jnp.dot(a, b, preferred_element_type=F32)


def _dot_nt(a, b):
    return lax.dot_general(a, b, (((1,), (1,)), ((), ())), preferred_element_type=F32)


def _dot_tn(a, b):
    return lax.dot_general(a, b, (((0,), (0,)), ((), ())), preferred_element_type=F32)


def _rms_scale(x):
    return lax.rsqrt(jnp.mean(x * x, axis=-1, keepdims=True) + EPS)


def _params(*sem):
    return pltpu.CompilerParams(dimension_semantics=sem, vmem_limit_bytes=VMEM_LIMIT)


def _gla_inproj_kernel(x_ref, g_ref, w_ref, wg2_ref, bg_ref, qkk_ref, vg_ref, eb_ref):
    tm = x_ref.shape[0]
    c = GLA_CHUNK
    sub = GLA_CUM_ROWS
    x = x_ref[...]
    h = (x * _rms_scale(x) * g_ref[...]).astype(BF16)
    low = _dot(h, w_ref[:, GLA_MAIN:GLA_MAIN + GLA_RANK])
    z = _dot(low.astype(BF16), wg2_ref[...]) + bg_ref[...]
    logg = (jnp.minimum(z, 0.0) - jnp.log(1.0 + jnp.exp(-jnp.abs(z)))) / GLA_GATE_NORM

    row = lax.broadcasted_iota(jnp.int32, (sub, sub), 0)
    col = lax.broadcasted_iota(jnp.int32, (sub, sub), 1)
    tri = jnp.where((row >= col) & (row // c == col // c), 1.0, 0.0).astype(BF16)
    lg_hi = logg.astype(BF16)
    lg_lo = (logg - lg_hi.astype(F32)).astype(BF16)
    split = jnp.concatenate([lg_hi, lg_lo], axis=1)
    cum = jnp.concatenate([_dot(tri, split[r * sub:(r + 1) * sub, :]) for r in range(tm // sub)], axis=0)
    bcum = cum[:, :GLA_QK] + cum[:, GLA_QK:]
    blast = [bcum[(n + 1) * c - 1:(n + 1) * c, :] for n in range(tm // c)]
    blast_rows = jnp.concatenate([jnp.broadcast_to(bl, (c, GLA_QK)) for bl in blast], axis=0)
    eb = jnp.exp(jnp.concatenate(blast, axis=0))

    q = _dot(h, w_ref[:, 0:GLA_QK]) * (GLA_DK ** -0.5)
    k = _dot(h, w_ref[:, GLA_QK:2 * GLA_QK])
    parts = (q * jnp.exp(bcum), k * jnp.exp(-bcum), k * jnp.exp(blast_rows - bcum))
    for i, part in enumerate(parts):
        part = part.astype(qkk_ref.dtype)
        for hd in range(GLA_HEADS):
            qkk_ref[i * GLA_HEADS + hd] = part[:, hd * GLA_DK:(hd + 1) * GLA_DK]
    for hd in range(GLA_HEADS):
        eb_ref[hd] = eb[:, hd * GLA_DK:(hd + 1) * GLA_DK]
    for cc in range(2 * GLA_HEADS):
        c0 = 2 * GLA_QK + cc * GLA_DV
        vg_ref[cc] = _dot(h, w_ref[:, c0:c0 + GLA_DV]).astype(vg_ref.dtype)


def _resident(shape):
    return pl.BlockSpec(shape, lambda i: (0,) * len(shape), pipeline_mode=pl.Buffered(1))


def _gla_inproj(x, g, w, w_g2, b_g, *, tm=512):
    m, d = x.shape
    return pl.pallas_call(
        _gla_inproj_kernel,
        grid=(m // tm,),
        in_specs=[
            pl.BlockSpec((tm, d), lambda i: (i, 0)),
            _resident((1, d)),
            _resident(w.shape),
            _resident(w_g2.shape),
            _resident((1, GLA_QK)),
        ],
        out_specs=[
            pl.BlockSpec((3 * GLA_HEADS, tm, GLA_DK), lambda i: (0, i, 0)),
            pl.BlockSpec((2 * GLA_HEADS, tm, GLA_DV), lambda i: (0, i, 0)),
            pl.BlockSpec((GLA_HEADS, tm // GLA_CHUNK, GLA_DK), lambda i: (0, i, 0)),
        ],
        out_shape=[
            jax.ShapeDtypeStruct((3 * GLA_HEADS, m, GLA_DK), BF16),
            jax.ShapeDtypeStruct((2 * GLA_HEADS, m, GLA_DV), BF16),
            jax.ShapeDtypeStruct((GLA_HEADS, m // GLA_CHUNK, GLA_DK), F32),
        ],
        compiler_params=_params("parallel"),
        name="gla_inproj",
    )(x, g, w, w_g2, b_g)


def _gla_core_kernel(qt_ref, kt_ref, ke_ref, v_ref, eb_ref, o_ref, st_scr, oi_scr, kv_scr, *, tc):
    seq = qt_ref.shape[0]
    c = GLA_CHUNK
    nchunk = tc // c
    nb = seq // tc
    chunks = [slice(n * c, (n + 1) * c) for n in range(nchunk)]
    row = lax.broadcasted_iota(jnp.int32, (tc, tc), 0)
    col = lax.broadcasted_iota(jnp.int32, (tc, tc), 1)
    causal = (row >= col) & (row // c == col // c)
    own_chunk = (lax.broadcasted_iota(jnp.int32, (tc, nchunk * GLA_DK), 0) // c
                 == lax.broadcasted_iota(jnp.int32, (tc, nchunk * GLA_DK), 1) // GLA_DK)

    def block_rows(t):
        return pl.ds(pl.multiple_of(t * tc, tc), tc)

    def products(t):
        slot = t % 2
        rows = block_rows(t)
        v = v_ref[rows, :]
        a = jnp.where(causal, _dot_nt(qt_ref[rows, :], kt_ref[rows, :]), 0.0).astype(BF16)
        oi_scr[slot] = _dot(a, v)
        ke = jnp.concatenate([ke_ref[rows, :]] * nchunk, axis=1)
        ke = jnp.where(own_chunk, ke, jnp.zeros_like(ke))
        kv_scr[slot] = _dot_tn(v, ke)

    def recur(t):
        slot = t % 2
        st = st_scr[...]
        for n, sl in enumerate(chunks):
            rows = pl.ds(pl.multiple_of(t * tc + n * c, c), c)
            o = oi_scr[slot, sl, :] + _dot_nt(qt_ref[rows, :], st.astype(BF16))
            o_ref[rows, :] = o.astype(o_ref.dtype)
            st = st * eb_ref[pl.ds(t * nchunk + n, 1), :] + kv_scr[slot, :, n * GLA_DK:(n + 1) * GLA_DK]
        st_scr[...] = st

    st_scr[...] = jnp.zeros_like(st_scr)
    products(0)

    def body(t, carry):
        recur(t)
        products(t + 1)
        return carry

    lax.fori_loop(0, nb - 1, body, 0)
    recur(nb - 1)


def _gla_core(qkk, vg, eb, *, batch, seq, tc=256):
    m = qkk.shape[1]
    nchunk = tc // GLA_CHUNK
    head_rows = lambda part, width: pl.BlockSpec((None, seq, width), lambda b, h: (part * GLA_HEADS + h, b, 0))
    return pl.pallas_call(
        functools.partial(_gla_core_kernel, tc=tc),
        grid=(batch, GLA_HEADS),
        in_specs=[head_rows(0, GLA_DK), head_rows(1, GLA_DK), head_rows(2, GLA_DK), head_rows(0, GLA_DV),
                  pl.BlockSpec((None, seq // GLA_CHUNK, GLA_DK), lambda b, h: (h, b, 0))],
        out_specs=pl.BlockSpec((None, seq, GLA_DV), lambda b, h: (h, b, 0)),
        out_shape=jax.ShapeDtypeStruct((GLA_HEADS, m, GLA_DV), BF16),
        scratch_shapes=[
            pltpu.VMEM((GLA_DV, GLA_DK), F32),
            pltpu.VMEM((2, tc, GLA_DV), F32),
            pltpu.VMEM((2, GLA_DV, nchunk * GLA_DK), F32),
        ],
        compiler_params=_params("parallel", "parallel"),
        name="gla_core",
    )(qkk, qkk, qkk, vg, eb)


def _finish_outproj(y, pg_ref, x_ref, o_ref):
    o_ref[...] = x_ref[...] + y * _rms_scale(y) * pg_ref[...]


def _outproj_kernel(a_ref, w_ref, pg_ref, x_ref, o_ref, *, kc):
    heads, _, width = a_ref.shape
    per = kc // width
    y = None
    for j in range(heads // per):
        a = jnp.concatenate([a_ref[j * per + i] for i in range(per)], axis=1)
        part = _dot(a, w_ref[j * kc:(j + 1) * kc, :])
        y = part if y is None else y + part
    _finish_outproj(y, pg_ref, x_ref, o_ref)


def _gla_outproj_kernel(a_ref, gate_ref, ng_ref, w_ref, pg_ref, x_ref, o_ref):
    heads, _, width = a_ref.shape
    y = None
    for hd in range(heads):
        o = a_ref[hd].astype(F32)
        gate = gate_ref[hd]
        silu = gate * (0.5 + 0.5 * jnp.tanh(0.5 * gate))
        a = (o * _rms_scale(o) * ng_ref[...]).astype(BF16) * silu
        part = _dot(a, w_ref[hd * width:(hd + 1) * width, :])
        y = part if y is None else y + part
    _finish_outproj(y, pg_ref, x_ref, o_ref)


def _outproj(a, w, post_g, x, *, gate=None, norm_g=None, tm=512, kc=256):
    heads, m, width = a.shape
    kdim, d = w.shape
    a_spec = pl.BlockSpec((heads, tm, width), lambda i: (0, i, 0))
    tail_specs = [
        pl.BlockSpec((kdim, d), lambda i: (0, 0)),
        pl.BlockSpec((1, d), lambda i: (0, 0)),
        pl.BlockSpec((tm, d), lambda i: (i, 0)),
    ]
    if gate is None:
        body, in_specs, args = functools.partial(_outproj_kernel, kc=kc), [a_spec], (a,)
    else:
        in_specs = [a_spec, pl.BlockSpec((heads, tm, width), lambda i: (1, i, 0)),
                    pl.BlockSpec((1, width), lambda i: (0, 0))]
        body, args = _gla_outproj_kernel, (a, gate, norm_g)
    return pl.pallas_call(
        body,
        grid=(m // tm,),
        in_specs=in_specs + tail_specs,
        out_specs=pl.BlockSpec((tm, d), lambda i: (i, 0)),
        out_shape=jax.ShapeDtypeStruct((m, d), F32),
        compiler_params=_params("parallel"),
        name="outproj",
    )(*args, w, post_g, x)


def _diff_inproj_kernel(x_ref, g_ref, w_ref, cos_ref, sin_ref, o_ref, vt_ref, *, tn):
    x = x_ref[...]
    h = (x * _rms_scale(x) * g_ref[...]).astype(BF16)
    cos_k, sin_k = cos_ref[...], sin_ref[...]
    cos_q, sin_q = cos_k * DIFF_Q_SCALE, sin_k * DIFF_Q_SCALE
    n_q = DIFF_BRANCH // tn
    per = tn // LANES
    tk = DIFF_KBLOCK
    for c in range(w_ref.shape[1] // tn):
        y = _dot(h, w_ref[:, c * tn:(c + 1) * tn])
        cos, sin = (cos_q, sin_q) if c < n_q else (cos_k, sin_k)
        for s in range(per):
            ysl = y[:, s * LANES:(s + 1) * LANES]
            if c < 2 * n_q:
                ysl = ysl * cos + pltpu.roll(ysl, LANES // 2, 1) * sin
                o_ref[c * per + s] = ysl.astype(o_ref.dtype)
            elif c < 3 * n_q:
                yt = ysl.T.astype(vt_ref.dtype)
                for r in range(yt.shape[1] // tk):
                    vt_ref[(c - 2 * n_q) * per + s, r] = yt[:, r * tk:(r + 1) * tk]
            else:
                o_ref[(c - n_q) * per + s] = ysl.astype(o_ref.dtype)


def _diff_inproj(x, g, w, cos, sin, *, seq, tm=512, tn=512):
    m, d = x.shape
    n = w.shape[1]
    tblocks = seq // tm
    return pl.pallas_call(
        functools.partial(_diff_inproj_kernel, tn=tn),
        grid=(m // tm,),
        in_specs=[
            pl.BlockSpec((tm, d), lambda i: (i, 0)),
            _resident((1, d)),
            _resident((d, n)),
            pl.BlockSpec((tm, LANES), lambda i: (i % tblocks, 0)),
            pl.BlockSpec((tm, LANES), lambda i: (i % tblocks, 0)),
        ],
        out_specs=[
            pl.BlockSpec((3 * DIFF_HEADS, tm, LANES), lambda i: (0, i, 0)),
            pl.BlockSpec((DIFF_HEADS, tm // DIFF_KBLOCK, LANES, DIFF_KBLOCK), lambda i: (0, i, 0, 0)),
        ],
        out_shape=[
            jax.ShapeDtypeStruct((3 * DIFF_HEADS, m, LANES), BF16),
            jax.ShapeDtypeStruct((DIFF_HEADS, m // DIFF_KBLOCK, LANES, DIFF_KBLOCK), BF16),
        ],
        compiler_params=_params("parallel"),
        name="diff_inproj",
    )(x, g, w, cos, sin)


def _rotary_lane_order(w_cols):
    d_in = w_cols.shape[0]
    half = DIFF_HEAD_DIM // 2
    w5 = w_cols.reshape(d_in, DIFF_HEADS, 2, 2, half)
    return w5.transpose(0, 1, 3, 2, 4).reshape(d_in, DIFF_BRANCH)


def _rope_tables(seq):
    d = DIFF_HEAD_DIM
    inv_freq = 1.0 / (ROPE_THETA ** (jnp.arange(0, d, 2, dtype=F32) / d))
    ang = jnp.arange(seq, dtype=F32)[:, None] * inv_freq[None, :]
    cos, sin = jnp.cos(ang), jnp.sin(ang)
    cos_l = jnp.concatenate([cos, cos, cos, cos], axis=-1)
    sin_l = jnp.concatenate([-sin, -sin, sin, sin], axis=-1)
    return cos_l, sin_l


def _diff_attn_kernel(lq1_ref, lk1_ref, lq2_ref, lk2_ref, q_ref, k_ref, v_ref, gate_ref, ng_ref, o_ref,
                      vt_scr, qs_scr, s_scr, pm_scr, p_scr, m_scr, alpha_scr, acc_scr, *, tq, lambda_init):
    seq = q_ref.shape[0]
    d = DIFF_HEAD_DIM
    tk = DIFF_KBLOCK
    unroll = tq // tk
    maps = (slice(0, tq), slice(tq, 2 * tq))

    ones_row = lax.broadcasted_iota(jnp.int32, (DIFF_ONES_ROWS, tk), 0) == 0
    for r in range(seq // tk):
        vt_scr[r, 0:2 * d, :] = v_ref[r]
        vt_scr[r, 2 * d:, :] = jnp.where(ones_row, 1.0, 0.0).astype(BF16)

    lam = (jnp.exp(jnp.sum(lq1_ref[...] * lk1_ref[...])) - jnp.exp(jnp.sum(lq2_ref[...] * lk2_ref[...]))
           + lambda_init)

    def scores(kstart, cs):
        return _dot(k_ref[pl.ds(kstart, tk), :], qs_scr[:, cs])

    def causal(s):
        kpos = lax.broadcasted_iota(jnp.int32, s.shape, 0)
        qpos = lax.broadcasted_iota(jnp.int32, s.shape, 1)
        return jnp.where(kpos <= qpos, s, -jnp.inf)

    def sublane_max(s):
        return jnp.max(s.reshape(s.shape[0] // 8, 8, s.shape[1]), axis=0)

    def softmax_step(s, cs, part_max=None):
        m_old = m_scr[:, cs]
        m_new = jnp.maximum(m_old, jnp.max(s if part_max is None else part_max, axis=0, keepdims=True))
        alpha = jnp.exp2(m_old - m_new)
        p = jnp.exp2(s - m_new)
        m_scr[:, cs] = m_new
        alpha_scr[:, cs] = alpha
        return p.astype(BF16)

    def value_step(p, vt, cs):
        acc_scr[:, cs] = alpha_scr[:, cs] * acc_scr[:, cs] + _dot(vt, p)

    def full_body(jj, carry):
        for u in range(unroll):
            j = jj * unroll + u
            vt_prev = vt_scr[jnp.maximum(j - 1, 0)]
            knext = pl.multiple_of((j + 1) * tk, tk)
            for mp in range(2):
                value_step(p_scr[mp], vt_prev, maps[mp])
                s_next = scores(knext, maps[mp])
                pm_next = sublane_max(s_next)
                p_scr[mp] = softmax_step(s_scr[mp], maps[mp], pm_scr[mp])
                s_scr[mp] = s_next
                pm_scr[mp] = pm_next
        return carry

    for qi in range(seq // tq):
        rows = slice(qi * tq, (qi + 1) * tq)
        qt = q_ref[rows, :].astype(F32).T
        row = lax.broadcasted_iota(jnp.int32, qt.shape, 0)
        is_a = (row // (d // 2)) % 2 == 0
        qs_scr[:, maps[0]] = jnp.where(is_a, qt, 0.0).astype(BF16)
        qs_scr[:, maps[1]] = jnp.where(is_a, 0.0, qt).astype(BF16)
        m_scr[...] = jnp.full_like(m_scr, -jnp.inf)
        acc_scr[...] = jnp.zeros_like(acc_scr)
        p_scr[...] = jnp.zeros_like(p_scr)
        alpha_scr[...] = jnp.ones_like(alpha_scr)

        for mp in range(2):
            s_first = scores(0, maps[mp])
            s_scr[mp] = s_first
            pm_scr[mp] = sublane_max(s_first)
        n_full = qi * unroll
        lax.fori_loop(0, qi, full_body, 0)

        d0 = qi * tq
        cols = [[slice(mp * tq + r * tk, (mp + 1) * tq) for mp in range(2)] for r in range(unroll)]
        s_cur = [s_scr[mp] for mp in range(2)]
        p_prev = [p_scr[mp] for mp in range(2)]
        vt_prev = vt_scr[max(n_full - 1, 0)]
        cols_prev = maps
        for r in range(unroll):
            s_nxt = [None, None]
            for mp in range(2):
                value_step(p_prev[mp], vt_prev, cols_prev[mp])
                if r + 1 < unroll:
                    s_nxt[mp] = scores(d0 + (r + 1) * tk, cols[r + 1][mp])
                p_prev[mp] = softmax_step(causal(s_cur[mp]), cols[r][mp])
            s_cur, vt_prev, cols_prev = s_nxt, vt_scr[n_full + r], cols[r]
        for mp in range(2):
            value_step(p_prev[mp], vt_prev, cols_prev[mp])

        inv_l = 1.0 / acc_scr[2 * d:2 * d + 1, :]
        o_t = (acc_scr[0:2 * d, maps[0]] * inv_l[:, maps[0]]
               - acc_scr[0:2 * d, maps[1]] * (lam * inv_l[:, maps[1]]))
        ms = jnp.sum(o_t * o_t, axis=0, keepdims=True) * (1.0 / (2 * d))
        o = (o_t * (lax.rsqrt(ms + EPS) * (1.0 - lambda_init))).T
        gate = gate_ref[rows, :]
        silu = gate * (0.5 + 0.5 * jnp.tanh(0.5 * gate))
        o_ref[rows, :] = (o * ng_ref[...]).astype(o_ref.dtype) * silu


def _diff_attn(proj, vt, lq1, lk1, lq2, lk2, norm_g, lambda_init, *, batch, seq, tq=1024):
    m = proj.shape[1]
    hd = 2 * DIFF_HEAD_DIM
    nkb = seq // DIFF_KBLOCK
    vec = pl.BlockSpec((1, DIFF_HEAD_DIM), lambda b, h: (0, 0))
    head_rows = lambda part: pl.BlockSpec((None, seq, hd), lambda b, h: (part * DIFF_HEADS + h, b, 0))
    return pl.pallas_call(
        functools.partial(_diff_attn_kernel, tq=tq, lambda_init=lambda_init),
        grid=(batch, DIFF_HEADS),
        in_specs=[vec, vec, vec, vec, head_rows(0), head_rows(1),
                  pl.BlockSpec((None, nkb, hd, DIFF_KBLOCK), lambda b, h: (h, b, 0, 0)),
                  head_rows(2),
                  pl.BlockSpec((1, hd), lambda b, h: (0, 0))],
        out_specs=pl.BlockSpec((None, seq, hd), lambda b, h: (h, b, 0)),
        out_shape=jax.ShapeDtypeStruct((DIFF_HEADS, m, hd), BF16),
        scratch_shapes=[
            pltpu.VMEM((seq // DIFF_KBLOCK, hd + DIFF_ONES_ROWS, DIFF_KBLOCK), BF16),
            pltpu.VMEM((hd, 2 * tq), BF16),
            pltpu.VMEM((2, DIFF_KBLOCK, tq), F32),
            pltpu.VMEM((2, 8, tq), F32),
            pltpu.VMEM((2, DIFF_KBLOCK, tq), BF16),
            pltpu.VMEM((1, 2 * tq), F32),
            pltpu.VMEM((1, 2 * tq), F32),
            pltpu.VMEM((hd + DIFF_ONES_ROWS, 2 * tq), F32),
        ],
        compiler_params=_params("parallel", "parallel"),
        name="diff_attn",
    )(lq1, lk1, lq2, lk2, proj, proj, vt, proj, norm_g)


def _gla_layer(x, pre_g, post_g, w_in, w_g2, b_g, norm_g, w_out, *, batch, seq):
    qkk, vg, eb = _gla_inproj(x, pre_g[None, :], w_in.astype(BF16), w_g2.astype(BF16), b_g[None, :])
    og = _gla_core(qkk, vg, eb, batch=batch, seq=seq)
    return _outproj(og, w_out.astype(BF16), post_g[None, :], x, gate=vg, norm_g=norm_g[None, :])


def _diff_layer(x, pre_g, post_g, w_in, lq1, lk1, lq2, lk2, norm_g, w_out, lambda_init, *, batch, seq):
    cos, sin = _rope_tables(seq)
    w_in = jnp.concatenate([_rotary_lane_order(w_in[:, :DIFF_BRANCH]),
                            _rotary_lane_order(w_in[:, DIFF_BRANCH:2 * DIFF_BRANCH]),
                            w_in[:, 2 * DIFF_BRANCH:]], axis=1).astype(BF16)
    proj, vt = _diff_inproj(x, pre_g[None, :], w_in, cos, sin, seq=seq)
    og = _diff_attn(proj, vt, lq1[None, :], lk1[None, :], lq2[None, :], lk2[None, :], norm_g[None, :],
                    lambda_init, batch=batch, seq=seq)
    return _outproj(og, w_out.astype(BF16), post_g[None, :], x)


def kernel(x, pre_g, post_g, gla_w_in, gla_w_g2, gla_b_g, gla_norm_g, gla_w_out, diff_w_in,
           diff_lam_q1, diff_lam_k1, diff_lam_q2, diff_lam_k2, diff_norm_g, diff_w_out):
    batch, seq, d_model = x.shape
    depth = pre_g.shape[0]
    xf = x.reshape(batch * seq, d_model)
    for i in range(depth):
        j = i // 2
        if i % 2 == 0:
            xf = _gla_layer(xf, pre_g[i], post_g[i], gla_w_in[j], gla_w_g2[j], gla_b_g[j], gla_norm_g[j],
                            gla_w_out[j], batch=batch, seq=seq)
        else:
            lambda_init = 0.8 - 0.6 * math.exp(-0.3 * i)
            xf = _diff_layer(xf, pre_g[i], post_g[i], diff_w_in[j], diff_lam_q1[j], diff_lam_k1[j],
                             diff_lam_q2[j], diff_lam_k2[j], diff_norm_g[j], diff_w_out[j], lambda_init,
                             batch=batch, seq=seq)
    return xf.reshape(batch, seq, d_model)
```

```python
import functools
import math

import jax
import jax.numpy as jnp
from jax import lax
from jax.experimental import pallas as pl
from jax.experimental.pallas import tpu as pltpu

F32 = jnp.float32
BF16 = jnp.bfloat16

EPS = 1e-6
LANES = 128

GLA_HEADS = 4
GLA_DK = 128
GLA_DV = 512
GLA_RANK = 16
GLA_GATE_NORM = 16.0
GLA_CHUNK = 64
GLA_CUM_ROWS = 256
GLA_QK = GLA_HEADS * GLA_DK
GLA_BRANCH = GLA_HEADS * GLA_DV
GLA_MAIN = 2 * GLA_QK + 2 * GLA_BRANCH

DIFF_HEAD_DIM = 64
DIFF_HEADS = 16
DIFF_BRANCH = DIFF_HEADS * 2 * DIFF_HEAD_DIM
DIFF_KBLOCK = 256
DIFF_ONES_ROWS = 16
DIFF_Q_SCALE = DIFF_HEAD_DIM ** -0.5 * math.log2(math.e)
ROPE_THETA = 10000.0

VMEM_LIMIT = 48 * 1024 * 1024


def _dot(a, b):
    return jnp.dot(a, b, preferred_element_type=F32)


def _dot_nt(a, b):
    return lax.dot_general(a, b, (((1,), (1,)), ((), ())), preferred_element_type=F32)


def _dot_tn(a, b):
    return lax.dot_general(a, b, (((0,), (0,)), ((), ())), preferred_element_type=F32)


def _rms_scale(x):
    return lax.rsqrt(jnp.mean(x * x, axis=-1, keepdims=True) + EPS)


def _params(*sem):
    return pltpu.CompilerParams(dimension_semantics=sem, vmem_limit_bytes=VMEM_LIMIT)


def _gla_inproj_kernel(x_ref, g_ref, w_ref, wg2_ref, bg_ref, qkk_ref, vg_ref, eb_ref):
    tm = x_ref.shape[0]
    c = GLA_CHUNK
    sub = GLA_CUM_ROWS
    x = x_ref[...]
    h = (x * _rms_scale(x) * g_ref[...]).astype(BF16)
    low = _dot(h, w_ref[:, GLA_MAIN:GLA_MAIN + GLA_RANK])
    z = _dot(low.astype(BF16), wg2_ref[...]) + bg_ref[...]
    logg = (jnp.minimum(z, 0.0) - jnp.log(1.0 + jnp.exp(-jnp.abs(z)))) / GLA_GATE_NORM

    row = lax.broadcasted_iota(jnp.int32, (sub, sub), 0)
    col = lax.broadcasted_iota(jnp.int32, (sub, sub), 1)
    tri = jnp.where((row >= col) & (row // c == col // c), 1.0, 0.0).astype(BF16)
    lg_hi = logg.astype(BF16)
    lg_lo = (logg - lg_hi.astype(F32)).astype(BF16)
    split = jnp.concatenate([lg_hi, lg_lo], axis=1)
    cum = jnp.concatenate([_dot(tri, split[r * sub:(r + 1) * sub, :]) for r in range(tm // sub)], axis=0)
    bcum = cum[:, :GLA_QK] + cum[:, GLA_QK:]
    blast = [bcum[(n + 1) * c - 1:(n + 1) * c, :] for n in range(tm // c)]
    blast_rows = jnp.concatenate([jnp.broadcast_to(bl, (c, GLA_QK)) for bl in blast], axis=0)
    eb = jnp.exp(jnp.concatenate(blast, axis=0))

    q = _dot(h, w_ref[:, 0:GLA_QK]) * (GLA_DK ** -0.5)
    k = _dot(h, w_ref[:, GLA_QK:2 * GLA_QK])
    parts = (q * jnp.exp(bcum), k * jnp.exp(-bcum), k * jnp.exp(blast_rows - bcum))
    for i, part in enumerate(parts):
        part = part.astype(qkk_ref.dtype)
        for hd in range(GLA_HEADS):
            qkk_ref[i * GLA_HEADS + hd] = part[:, hd * GLA_DK:(hd + 1) * GLA_DK]
    for hd in range(GLA_HEADS):
        eb_ref[hd] = eb[:, hd * GLA_DK:(hd + 1) * GLA_DK]
    for cc in range(2 * GLA_HEADS):
        c0 = 2 * GLA_QK + cc * GLA_DV
        vg_ref[cc] = _dot(h, w_ref[:, c0:c0 + GLA_DV]).astype(vg_ref.dtype)


def _resident(shape):
    return pl.BlockSpec(shape, lambda i: (0,) * len(shape), pipeline_mode=pl.Buffered(1))


def _gla_inproj(x, g, w, w_g2, b_g, *, tm=512):
    m, d = x.shape
    return pl.pallas_call(
        _gla_inproj_kernel,
        grid=(m // tm,),
        in_specs=[
            pl.BlockSpec((tm, d), lambda i: (i, 0)),
            _resident((1, d)),
            _resident(w.shape),
            _resident(w_g2.shape),
            _resident((1, GLA_QK)),
        ],
        out_specs=[
            pl.BlockSpec((3 * GLA_HEADS, tm, GLA_DK), lambda i: (0, i, 0)),
            pl.BlockSpec((2 * GLA_HEADS, tm, GLA_DV), lambda i: (0, i, 0)),
            pl.BlockSpec((GLA_HEADS, tm // GLA_CHUNK, GLA_DK), lambda i: (0, i, 0)),
        ],
        out_shape=[
            jax.ShapeDtypeStruct((3 * GLA_HEADS, m, GLA_DK), BF16),
            jax.ShapeDtypeStruct((2 * GLA_HEADS, m, GLA_DV), BF16),
            jax.ShapeDtypeStruct((GLA_HEADS, m // GLA_CHUNK, GLA_DK), F32),
        ],
        compiler_params=_params("parallel"),
        name="gla_inproj",
    )(x, g, w, w_g2, b_g)


def _gla_core_kernel(qt_ref, kt_ref, ke_ref, v_ref, eb_ref, o_ref, st_scr, oi_scr, kv_scr, *, tc):
    seq = qt_ref.shape[0]
    c = GLA_CHUNK
    nchunk = tc // c
    nb = seq // tc
    chunks = [slice(n * c, (n + 1) * c) for n in range(nchunk)]
    row = lax.broadcasted_iota(jnp.int32, (tc, tc), 0)
    col = lax.broadcasted_iota(jnp.int32, (tc, tc), 1)
    causal = (row >= col) & (row // c == col // c)
    own_chunk = (lax.broadcasted_iota(jnp.int32, (tc, nchunk * GLA_DK), 0) // c
                 == lax.broadcasted_iota(jnp.int32, (tc, nchunk * GLA_DK), 1) // GLA_DK)

    def block_rows(t):
        return pl.ds(pl.multiple_of(t * tc, tc), tc)

    def products(t):
        slot = t % 2
        rows = block_rows(t)
        v = v_ref[rows, :]
        a = jnp.where(causal, _dot_nt(qt_ref[rows, :], kt_ref[rows, :]), 0.0).astype(BF16)
        oi_scr[slot] = _dot(a, v)
        ke = jnp.concatenate([ke_ref[rows, :]] * nchunk, axis=1)
        ke = jnp.where(own_chunk, ke, jnp.zeros_like(ke))
        kv_scr[slot] = _dot_tn(v, ke)

    def recur(t):
        slot = t % 2
        st = st_scr[...]
        for n, sl in enumerate(chunks):
            rows = pl.ds(pl.multiple_of(t * tc + n * c, c), c)
            o = oi_scr[slot, sl, :] + _dot_nt(qt_ref[rows, :], st.astype(BF16))
            o_ref[rows, :] = o.astype(o_ref.dtype)
            st = st * eb_ref[pl.ds(t * nchunk + n, 1), :] + kv_scr[slot, :, n * GLA_DK:(n + 1) * GLA_DK]
        st_scr[...] = st

    st_scr[...] = jnp.zeros_like(st_scr)
    products(0)

    def body(t, carry):
        recur(t)
        products(t + 1)
        return carry

    lax.fori_loop(0, nb - 1, body, 0)
    recur(nb - 1)


def _gla_core(qkk, vg, eb, *, batch, seq, tc=256):
    m = qkk.shape[1]
    nchunk = tc // GLA_CHUNK
    head_rows = lambda part, width: pl.BlockSpec((None, seq, width), lambda b, h: (part * GLA_HEADS + h, b, 0))
    return pl.pallas_call(
        functools.partial(_gla_core_kernel, tc=tc),
        grid=(batch, GLA_HEADS),
        in_specs=[head_rows(0, GLA_DK), head_rows(1, GLA_DK), head_rows(2, GLA_DK), head_rows(0, GLA_DV),
                  pl.BlockSpec((None, seq // GLA_CHUNK, GLA_DK), lambda b, h: (h, b, 0))],
        out_specs=pl.BlockSpec((None, seq, GLA_DV), lambda b, h: (h, b, 0)),
        out_shape=jax.ShapeDtypeStruct((GLA_HEADS, m, GLA_DV), BF16),
        scratch_shapes=[
            pltpu.VMEM((GLA_DV, GLA_DK), F32),
            pltpu.VMEM((2, tc, GLA_DV), F32),
            pltpu.VMEM((2, GLA_DV, nchunk * GLA_DK), F32),
        ],
        compiler_params=_params("parallel", "parallel"),
        name="gla_core",
    )(qkk, qkk, qkk, vg, eb)


def _finish_outproj(y, pg_ref, x_ref, o_ref):
    o_ref[...] = x_ref[...] + y * _rms_scale(y) * pg_ref[...]


def _outproj_kernel(a_ref, w_ref, pg_ref, x_ref, o_ref, *, kc):
    heads, _, width = a_ref.shape
    per = kc // width
    y = None
    for j in range(heads // per):
        a = jnp.concatenate([a_ref[j * per + i] for i in range(per)], axis=1)
        part = _dot(a, w_ref[j * kc:(j + 1) * kc, :])
        y = part if y is None else y + part
    _finish_outproj(y, pg_ref, x_ref, o_ref)


def _gla_outproj_kernel(a_ref, gate_ref, ng_ref, w_ref, pg_ref, x_ref, o_ref):
    heads, _, width = a_ref.shape
    y = None
    for hd in range(heads):
        o = a_ref[hd].astype(F32)
        gate = gate_ref[hd]
        silu = gate * (0.5 + 0.5 * jnp.tanh(0.5 * gate))
        a = (o * _rms_scale(o) * ng_ref[...]).astype(BF16) * silu
        part = _dot(a, w_ref[hd * width:(hd + 1) * width, :])
        y = part if y is None else y + part
    _finish_outproj(y, pg_ref, x_ref, o_ref)


def _outproj(a, w, post_g, x, *, gate=None, norm_g=None, tm=512, kc=256):
    heads, m, width = a.shape
    kdim, d = w.shape
    a_spec = pl.BlockSpec((heads, tm, width), lambda i: (0, i, 0))
    tail_specs = [
        pl.BlockSpec((kdim, d), lambda i: (0, 0)),
        pl.BlockSpec((1, d), lambda i: (0, 0)),
        pl.BlockSpec((tm, d), lambda i: (i, 0)),
    ]
    if gate is None:
        body, in_specs, args = functools.partial(_outproj_kernel, kc=kc), [a_spec], (a,)
    else:
        in_specs = [a_spec, pl.BlockSpec((heads, tm, width), lambda i: (1, i, 0)),
                    pl.BlockSpec((1, width), lambda i: (0, 0))]
        body, args = _gla_outproj_kernel, (a, gate, norm_g)
    return pl.pallas_call(
        body,
        grid=(m // tm,),
        in_specs=in_specs + tail_specs,
        out_specs=pl.BlockSpec((tm, d), lambda i: (i, 0)),
        out_shape=jax.ShapeDtypeStruct((m, d), F32),
        compiler_params=_params("parallel"),
        name="outproj",
    )(*args, w, post_g, x)


def _diff_inproj_kernel(x_ref, g_ref, w_ref, cos_ref, sin_ref, o_ref, vt_ref, *, tn):
    x = x_ref[...]
    h = (x * _rms_scale(x) * g_ref[...]).astype(BF16)
    cos_k, sin_k = cos_ref[...], sin_ref[...]
    cos_q, sin_q = cos_k * DIFF_Q_SCALE, sin_k * DIFF_Q_SCALE
    n_q = DIFF_BRANCH // tn
    per = tn // LANES
    tk = DIFF_KBLOCK
    for c in range(w_ref.shape[1] // tn):
        y = _dot(h, w_ref[:, c * tn:(c + 1) * tn])
        cos, sin = (cos_q, sin_q) if c < n_q else (cos_k, sin_k)
        for s in range(per):
            ysl = y[:, s * LANES:(s + 1) * LANES]
            if c < 2 * n_q:
                ysl = ysl * cos + pltpu.roll(ysl, LANES // 2, 1) * sin
                o_ref[c * per + s] = ysl.astype(o_ref.dtype)
            elif c < 3 * n_q:
                yt = ysl.T.astype(vt_ref.dtype)
                for r in range(yt.shape[1] // tk):
                    vt_ref[(c - 2 * n_q) * per + s, r] = yt[:, r * tk:(r + 1) * tk]
            else:
                o_ref[(c - n_q) * per + s] = ysl.astype(o_ref.dtype)


def _diff_inproj(x, g, w, cos, sin, *, seq, tm=512, tn=512):
    m, d = x.shape
    n = w.shape[1]
    tblocks = seq // tm
    return pl.pallas_call(
        functools.partial(_diff_inproj_kernel, tn=tn),
        grid=(m // tm,),
        in_specs=[
            pl.BlockSpec((tm, d), lambda i: (i, 0)),
            _resident((1, d)),
            _resident((d, n)),
            pl.BlockSpec((tm, LANES), lambda i: (i % tblocks, 0)),
            pl.BlockSpec((tm, LANES), lambda i: (i % tblocks, 0)),
        ],
        out_specs=[
            pl.BlockSpec((3 * DIFF_HEADS, tm, LANES), lambda i: (0, i, 0)),
            pl.BlockSpec((DIFF_HEADS, tm // DIFF_KBLOCK, LANES, DIFF_KBLOCK), lambda i: (0, i, 0, 0)),
        ],
        out_shape=[
            jax.ShapeDtypeStruct((3 * DIFF_HEADS, m, LANES), BF16),
            jax.ShapeDtypeStruct((DIFF_HEADS, m // DIFF_KBLOCK, LANES, DIFF_KBLOCK), BF16),
        ],
        compiler_params=_params("parallel"),
        name="diff_inproj",
    )(x, g, w, cos, sin)


def _rotary_weight_kernel(w_ref, o_ref, *, n_rot):
    quarter = LANES // 4

    @pl.when(pl.program_id(0) < n_rot)
    def _():
        lane = lax.broadcasted_iota(jnp.int32, (w_ref.shape[0], LANES), 1)
        for s in range(w_ref.shape[1] // LANES):
            cols = slice(s * LANES, (s + 1) * LANES)
            w = w_ref[:, cols]
            from_right = pltpu.roll(w, LANES - quarter, 1)
            from_left = pltpu.roll(w, quarter, 1)
            w = jnp.where(lane // quarter == 1, from_right, jnp.where(lane // quarter == 2, from_left, w))
            o_ref[:, cols] = w.astype(o_ref.dtype)

    @pl.when(pl.program_id(0) >= n_rot)
    def _():
        o_ref[...] = w_ref[...].astype(o_ref.dtype)


def _rotary_weight(w, *, tn=512):
    d, n = w.shape
    return pl.pallas_call(
        functools.partial(_rotary_weight_kernel, n_rot=2 * DIFF_BRANCH // tn),
        grid=(n // tn,),
        in_specs=[pl.BlockSpec((d, tn), lambda c: (0, c))],
        out_specs=pl.BlockSpec((d, tn), lambda c: (0, c)),
        out_shape=jax.ShapeDtypeStruct((d, n), BF16),
        compiler_params=_params("parallel"),
        name="diff_wprep",
    )(w)


def _rope_tables(seq):
    d = DIFF_HEAD_DIM
    step = 64
    inv_freq = 1.0 / (ROPE_THETA ** (jnp.arange(0, d, 2, dtype=F32) / d))
    lo = jnp.arange(step, dtype=F32)[:, None] * inv_freq[None, :]
    hi = (jnp.arange(seq // step, dtype=F32) * step)[:, None] * inv_freq[None, :]
    cl, sl, ch, sh = jnp.cos(lo)[None], jnp.sin(lo)[None], jnp.cos(hi)[:, None], jnp.sin(hi)[:, None]
    cos = (ch * cl - sh * sl).reshape(seq, d // 2)
    sin = (sh * cl + ch * sl).reshape(seq, d // 2)
    cos_l = jnp.concatenate([cos, cos, cos, cos], axis=-1)
    sin_l = jnp.concatenate([-sin, -sin, sin, sin], axis=-1)
    return cos_l, sin_l


def _diff_attn_kernel(lq1_ref, lk1_ref, lq2_ref, lk2_ref, q_ref, k_ref, v_ref, gate_ref, ng_ref, o_ref,
                      vt_scr, qs_scr, s_scr, pm_scr, p_scr, m_scr, alpha_scr, acc_scr, *, tq, lambda_init):
    seq = q_ref.shape[0]
    d = DIFF_HEAD_DIM
    tk = DIFF_KBLOCK
    unroll = tq // tk
    maps = (slice(0, tq), slice(tq, 2 * tq))

    ones_row = lax.broadcasted_iota(jnp.int32, (DIFF_ONES_ROWS, tk), 0) == 0
    for r in range(seq // tk):
        vt_scr[r, 0:2 * d, :] = v_ref[r]
        vt_scr[r, 2 * d:, :] = jnp.where(ones_row, 1.0, 0.0).astype(BF16)

    lam = (jnp.exp(jnp.sum(lq1_ref[...] * lk1_ref[...])) - jnp.exp(jnp.sum(lq2_ref[...] * lk2_ref[...]))
           + lambda_init)

    def scores(kstart, cs):
        return _dot(k_ref[pl.ds(kstart, tk), :], qs_scr[:, cs])

    def causal(s):
        kpos = lax.broadcasted_iota(jnp.int32, s.shape, 0)
        qpos = lax.broadcasted_iota(jnp.int32, s.shape, 1)
        return jnp.where(kpos <= qpos, s, -jnp.inf)

    def sublane_max(s):
        return jnp.max(s.reshape(s.shape[0] // 8, 8, s.shape[1]), axis=0)

    def softmax_step(s, cs, part_max=None):
        m_old = m_scr[:, cs]
        m_new = jnp.maximum(m_old, jnp.max(s if part_max is None else part_max, axis=0, keepdims=True))
        alpha = jnp.exp2(m_old - m_new)
        p = jnp.exp2(s - m_new)
        m_scr[:, cs] = m_new
        alpha_scr[:, cs] = alpha
        return p.astype(BF16)

    def value_step(p, vt, cs):
        acc_scr[:, cs] = alpha_scr[:, cs] * acc_scr[:, cs] + _dot(vt, p)

    def full_body(jj, carry):
        for u in range(unroll):
            j = jj * unroll + u
            vt_prev = vt_scr[jnp.maximum(j - 1, 0)]
            knext = pl.multiple_of((j + 1) * tk, tk)
            for mp in range(2):
                value_step(p_scr[mp], vt_prev, maps[mp])
                s_next = scores(knext, maps[mp])
                pm_next = sublane_max(s_next)
                p_scr[mp] = softmax_step(s_scr[mp], maps[mp], pm_scr[mp])
                s_scr[mp] = s_next
                pm_scr[mp] = pm_next
        return carry

    for qi in range(seq // tq):
        rows = slice(qi * tq, (qi + 1) * tq)
        qt = q_ref[rows, :].astype(F32).T
        row = lax.broadcasted_iota(jnp.int32, qt.shape, 0)
        is_a = (row // (d // 2)) % 2 == 0
        qs_scr[:, maps[0]] = jnp.where(is_a, qt, 0.0).astype(BF16)
        qs_scr[:, maps[1]] = jnp.where(is_a, 0.0, qt).astype(BF16)
        m_scr[...] = jnp.full_like(m_scr, -jnp.inf)
        acc_scr[...] = jnp.zeros_like(acc_scr)
        p_scr[...] = jnp.zeros_like(p_scr)
        alpha_scr[...] = jnp.ones_like(alpha_scr)

        for mp in range(2):
            s_first = scores(0, maps[mp])
            s_scr[mp] = s_first
            pm_scr[mp] = sublane_max(s_first)
        n_full = qi * unroll
        lax.fori_loop(0, qi, full_body, 0)

        d0 = qi * tq
        cols = [[slice(mp * tq + r * tk, (mp + 1) * tq) for mp in range(2)] for r in range(unroll)]
        s_cur = [s_scr[mp] for mp in range(2)]
        p_prev = [p_scr[mp] for mp in range(2)]
        vt_prev = vt_scr[max(n_full - 1, 0)]
        cols_prev = maps
        for r in range(unroll):
            s_nxt = [None, None]
            for mp in range(2):
                value_step(p_prev[mp], vt_prev, cols_prev[mp])
                if r + 1 < unroll:
                    s_nxt[mp] = scores(d0 + (r + 1) * tk, cols[r + 1][mp])
                p_prev[mp] = softmax_step(causal(s_cur[mp]), cols[r][mp])
            s_cur, vt_prev, cols_prev = s_nxt, vt_scr[n_full + r], cols[r]
        for mp in range(2):
            value_step(p_prev[mp], vt_prev, cols_prev[mp])

        inv_l = 1.0 / acc_scr[2 * d:2 * d + 1, :]
        o_t = (acc_scr[0:2 * d, maps[0]] * inv_l[:, maps[0]]
               - acc_scr[0:2 * d, maps[1]] * (lam * inv_l[:, maps[1]]))
        ms = jnp.sum(o_t * o_t, axis=0, keepdims=True) * (1.0 / (2 * d))
        o = (o_t * (lax.rsqrt(ms + EPS) * (1.0 - lambda_init))).T
        gate = gate_ref[rows, :]
        silu = gate * (0.5 + 0.5 * jnp.tanh(0.5 * gate))
        o_ref[rows, :] = (o * ng_ref[...]).astype(o_ref.dtype) * silu


def _diff_attn(proj, vt, lq1, lk1, lq2, lk2, norm_g, lambda_init, *, batch, seq, tq=1024):
    m = proj.shape[1]
    hd = 2 * DIFF_HEAD_DIM
    nkb = seq // DIFF_KBLOCK
    vec = pl.BlockSpec((1, DIFF_HEAD_DIM), lambda b, h: (0, 0))
    head_rows = lambda part: pl.BlockSpec((None, seq, hd), lambda b, h: (part * DIFF_HEADS + h, b, 0))
    return pl.pallas_call(
        functools.partial(_diff_attn_kernel, tq=tq, lambda_init=lambda_init),
        grid=(batch, DIFF_HEADS),
        in_specs=[vec, vec, vec, vec, head_rows(0), head_rows(1),
                  pl.BlockSpec((None, nkb, hd, DIFF_KBLOCK), lambda b, h: (h, b, 0, 0)),
                  head_rows(2),
                  pl.BlockSpec((1, hd), lambda b, h: (0, 0))],
        out_specs=pl.BlockSpec((None, seq, hd), lambda b, h: (h, b, 0)),
        out_shape=jax.ShapeDtypeStruct((DIFF_HEADS, m, hd), BF16),
        scratch_shapes=[
            pltpu.VMEM((seq // DIFF_KBLOCK, hd + DIFF_ONES_ROWS, DIFF_KBLOCK), BF16),
            pltpu.VMEM((hd, 2 * tq), BF16),
            pltpu.VMEM((2, DIFF_KBLOCK, tq), F32),
            pltpu.VMEM((2, 8, tq), F32),
            pltpu.VMEM((2, DIFF_KBLOCK, tq), BF16),
            pltpu.VMEM((1, 2 * tq), F32),
            pltpu.VMEM((1, 2 * tq), F32),
            pltpu.VMEM((hd + DIFF_ONES_ROWS, 2 * tq), F32),
        ],
        compiler_params=_params("parallel", "parallel"),
        name="diff_attn",
    )(lq1, lk1, lq2, lk2, proj, proj, vt, proj, norm_g)


def _gla_layer(x, pre_g, post_g, w_in, w_g2, b_g, norm_g, w_out, *, batch, seq):
    qkk, vg, eb = _gla_inproj(x, pre_g[None, :], w_in.astype(BF16), w_g2.astype(BF16), b_g[None, :])
    og = _gla_core(qkk, vg, eb, batch=batch, seq=seq)
    return _outproj(og, w_out.astype(BF16), post_g[None, :], x, gate=vg, norm_g=norm_g[None, :])


def _diff_layer(x, pre_g, post_g, w_in, lq1, lk1, lq2, lk2, norm_g, w_out, lambda_init, *, batch, seq):
    cos, sin = _rope_tables(seq)
    proj, vt = _diff_inproj(x, pre_g[None, :], _rotary_weight(w_in), cos, sin, seq=seq)
    og = _diff_attn(proj, vt, lq1[None, :], lk1[None, :], lq2[None, :], lk2[None, :], norm_g[None, :],
                    lambda_init, batch=batch, seq=seq)
    return _outproj(og, w_out.astype(BF16), post_g[None, :], x)


def kernel(x, pre_g, post_g, gla_w_in, gla_w_g2, gla_b_g, gla_norm_g, gla_w_out, diff_w_in,
           diff_lam_q1, diff_lam_k1, diff_lam_q2, diff_lam_k2, diff_norm_g, diff_w_out):
    batch, seq, d_model = x.shape
    depth = pre_g.shape[0]
    xf = x.reshape(batch * seq, d_model)
    for i in range(depth):
        j = i // 2
        if i % 2 == 0:
            xf = _gla_layer(xf, pre_g[i], post_g[i], gla_w_in[j], gla_w_g2[j], gla_b_g[j], gla_norm_g[j],
                            gla_w_out[j], batch=batch, seq=seq)
        else:
            lambda_init = 0.8 - 0.6 * math.exp(-0.3 * i)
            xf = _diff_layer(xf, pre_g[i], post_g[i], diff_w_in[j], diff_lam_q1[j], diff_lam_k1[j],
                             diff_lam_q2[j], diff_lam_k2[j], diff_norm_g[j], diff_w_out[j], lambda_init,
                             batch=batch, seq=seq)
    return xf.reshape(batch, seq, d_model)
```

```python
import functools
import math

import jax
import jax.numpy as jnp
from jax import lax
from jax.experimental import pallas as pl
from jax.experimental.pallas import tpu as pltpu

F32 = jnp.float32
BF16 = jnp.bfloat16

EPS = 1e-6
LANES = 128
SUBLANES = 8

GLA_HEADS = 4
GLA_DK = 128
GLA_DV = 512
GLA_RANK = 16
GLA_GATE_NORM = 16.0
GLA_CHUNK = 64
GLA_CUM_ROWS = 256
GLA_QK = GLA_HEADS * GLA_DK
GLA_BRANCH = GLA_HEADS * GLA_DV
GLA_MAIN = 2 * GLA_QK + 2 * GLA_BRANCH

DIFF_HEAD_DIM = 64
DIFF_HEADS = 16
DIFF_BRANCH = DIFF_HEADS * 2 * DIFF_HEAD_DIM
DIFF_KBLOCK = 256
DIFF_ONES_ROWS = 16
DIFF_Q_SCALE = DIFF_HEAD_DIM ** -0.5 * math.log2(math.e)
ROPE_THETA = 10000.0

VMEM_LIMIT = 48 * 1024 * 1024


def _dot(a, b):
    return jnp.dot(a, b, preferred_element_type=F32)


def _dot_nt(a, b):
    return lax.dot_general(a, b, (((1,), (1,)), ((), ())), preferred_element_type=F32)


def _dot_tn(a, b):
    return lax.dot_general(a, b, (((0,), (0,)), ((), ())), preferred_element_type=F32)


def _rms_scale(x):
    return lax.rsqrt(jnp.mean(x * x, axis=-1, keepdims=True) + EPS)


def _params(*sem):
    return pltpu.CompilerParams(dimension_semantics=sem, vmem_limit_bytes=VMEM_LIMIT)


def _gla_inproj_kernel(x_ref, g_ref, w_ref, wg2_ref, bg_ref, qkk_ref, vg_ref, eb_ref):
    tm = x_ref.shape[0]
    c = GLA_CHUNK
    sub = GLA_CUM_ROWS
    x = x_ref[...]
    h = (x * _rms_scale(x) * g_ref[...]).astype(BF16)
    low = _dot(h, w_ref[:, GLA_MAIN:GLA_MAIN + GLA_RANK])
    z = _dot(low.astype(BF16), wg2_ref[...]) + bg_ref[...]
    logg = (jnp.minimum(z, 0.0) - jnp.log(1.0 + jnp.exp(-jnp.abs(z)))) / GLA_GATE_NORM

    row = lax.broadcasted_iota(jnp.int32, (sub, sub), 0)
    col = lax.broadcasted_iota(jnp.int32, (sub, sub), 1)
    tri = jnp.where((row >= col) & (row // c == col // c), 1.0, 0.0).astype(BF16)
    lg_hi = logg.astype(BF16)
    lg_lo = (logg - lg_hi.astype(F32)).astype(BF16)
    split = jnp.concatenate([lg_hi, lg_lo], axis=1)
    cum = jnp.concatenate([_dot(tri, split[r * sub:(r + 1) * sub, :]) for r in range(tm // sub)], axis=0)
    bcum = cum[:, :GLA_QK] + cum[:, GLA_QK:]
    blast = [bcum[(n + 1) * c - 1:(n + 1) * c, :] for n in range(tm // c)]
    blast_rows = jnp.concatenate([jnp.broadcast_to(bl, (c, GLA_QK)) for bl in blast], axis=0)
    eb = jnp.exp(jnp.concatenate(blast, axis=0))

    q = _dot(h, w_ref[:, 0:GLA_QK]) * (GLA_DK ** -0.5)
    k = _dot(h, w_ref[:, GLA_QK:2 * GLA_QK])
    parts = (q * jnp.exp(bcum), k * jnp.exp(-bcum), k * jnp.exp(blast_rows - bcum))
    for i, part in enumerate(parts):
        part = part.astype(qkk_ref.dtype)
        for hd in range(GLA_HEADS):
            qkk_ref[i * GLA_HEADS + hd] = part[:, hd * GLA_DK:(hd + 1) * GLA_DK]
    for hd in range(GLA_HEADS):
        eb_ref[hd] = eb[:, hd * GLA_DK:(hd + 1) * GLA_DK]
    for cc in range(2 * GLA_HEADS):
        c0 = 2 * GLA_QK + cc * GLA_DV
        vg_ref[cc] = _dot(h, w_ref[:, c0:c0 + GLA_DV]).astype(vg_ref.dtype)


def _resident(shape):
    return pl.BlockSpec(shape, lambda i: (0,) * len(shape), pipeline_mode=pl.Buffered(1))


def _gla_inproj(x, g, w, w_g2, b_g, *, tm=512):
    m, d = x.shape
    return pl.pallas_call(
        _gla_inproj_kernel,
        grid=(m // tm,),
        in_specs=[
            pl.BlockSpec((tm, d), lambda i: (i, 0)),
            _resident((1, d)),
            _resident(w.shape),
            _resident(w_g2.shape),
            _resident((1, GLA_QK)),
        ],
        out_specs=[
            pl.BlockSpec((3 * GLA_HEADS, tm, GLA_DK), lambda i: (0, i, 0)),
            pl.BlockSpec((2 * GLA_HEADS, tm, GLA_DV), lambda i: (0, i, 0)),
            pl.BlockSpec((GLA_HEADS, tm // GLA_CHUNK, GLA_DK), lambda i: (0, i, 0)),
        ],
        out_shape=[
            jax.ShapeDtypeStruct((3 * GLA_HEADS, m, GLA_DK), BF16),
            jax.ShapeDtypeStruct((2 * GLA_HEADS, m, GLA_DV), BF16),
            jax.ShapeDtypeStruct((GLA_HEADS, m // GLA_CHUNK, GLA_DK), F32),
        ],
        compiler_params=_params("parallel"),
        name="gla_inproj",
    )(x, g, w, w_g2, b_g)


def _gla_core_kernel(qt_ref, kt_ref, ke_ref, v_ref, eb_ref, o_ref, st_scr, oi_scr, kv_scr, *, tc):
    seq = qt_ref.shape[0]
    c = GLA_CHUNK
    nchunk = tc // c
    nb = seq // tc
    chunks = [slice(n * c, (n + 1) * c) for n in range(nchunk)]
    row = lax.broadcasted_iota(jnp.int32, (tc, tc), 0)
    col = lax.broadcasted_iota(jnp.int32, (tc, tc), 1)
    causal = (row >= col) & (row // c == col // c)
    own_chunk = (lax.broadcasted_iota(jnp.int32, (tc, nchunk * GLA_DK), 0) // c
                 == lax.broadcasted_iota(jnp.int32, (tc, nchunk * GLA_DK), 1) // GLA_DK)

    def block_rows(t):
        return pl.ds(pl.multiple_of(t * tc, tc), tc)

    def products(t):
        slot = t % 2
        rows = block_rows(t)
        v = v_ref[rows, :]
        a = jnp.where(causal, _dot_nt(qt_ref[rows, :], kt_ref[rows, :]), 0.0).astype(BF16)
        oi_scr[slot] = _dot(a, v)
        ke = jnp.concatenate([ke_ref[rows, :]] * nchunk, axis=1)
        ke = jnp.where(own_chunk, ke, jnp.zeros_like(ke))
        kv_scr[slot] = _dot_tn(v, ke)

    def recur(t):
        slot = t % 2
        st = st_scr[...]
        for n, sl in enumerate(chunks):
            rows = pl.ds(pl.multiple_of(t * tc + n * c, c), c)
            o = oi_scr[slot, sl, :] + _dot_nt(qt_ref[rows, :], st.astype(BF16))
            o_ref[rows, :] = o.astype(o_ref.dtype)
            st = st * eb_ref[pl.ds(t * nchunk + n, 1), :] + kv_scr[slot, :, n * GLA_DK:(n + 1) * GLA_DK]
        st_scr[...] = st

    st_scr[...] = jnp.zeros_like(st_scr)
    products(0)

    def body(t, carry):
        recur(t)
        products(t + 1)
        return carry

    lax.fori_loop(0, nb - 1, body, 0)
    recur(nb - 1)


def _gla_core(qkk, vg, eb, *, batch, seq, tc=256):
    m = qkk.shape[1]
    nchunk = tc // GLA_CHUNK
    head_rows = lambda part, width: pl.BlockSpec((None, seq, width), lambda b, h: (part * GLA_HEADS + h, b, 0))
    return pl.pallas_call(
        functools.partial(_gla_core_kernel, tc=tc),
        grid=(batch, GLA_HEADS),
        in_specs=[head_rows(0, GLA_DK), head_rows(1, GLA_DK), head_rows(2, GLA_DK), head_rows(0, GLA_DV),
                  pl.BlockSpec((None, seq // GLA_CHUNK, GLA_DK), lambda b, h: (h, b, 0))],
        out_specs=pl.BlockSpec((None, seq, GLA_DV), lambda b, h: (h, b, 0)),
        out_shape=jax.ShapeDtypeStruct((GLA_HEADS, m, GLA_DV), BF16),
        scratch_shapes=[
            pltpu.VMEM((GLA_DV, GLA_DK), F32),
            pltpu.VMEM((2, tc, GLA_DV), F32),
            pltpu.VMEM((2, GLA_DV, nchunk * GLA_DK), F32),
        ],
        compiler_params=_params("parallel", "parallel"),
        name="gla_core",
    )(qkk, qkk, qkk, vg, eb)


def _finish_outproj(y, pg_ref, x_ref, o_ref):
    o_ref[...] = x_ref[...] + y * _rms_scale(y) * pg_ref[...]


def _outproj_kernel(a_ref, w_ref, pg_ref, x_ref, o_ref, *, kc):
    heads, _, width = a_ref.shape
    per = kc // width
    y = None
    for j in range(heads // per):
        a = jnp.concatenate([a_ref[j * per + i] for i in range(per)], axis=1)
        part = _dot(a, w_ref[j * kc:(j + 1) * kc, :])
        y = part if y is None else y + part
    _finish_outproj(y, pg_ref, x_ref, o_ref)


def _gla_outproj_kernel(a_ref, gate_ref, ng_ref, w_ref, pg_ref, x_ref, o_ref):
    heads, _, width = a_ref.shape
    y = None
    for hd in range(heads):
        o = a_ref[hd].astype(F32)
        gate = gate_ref[hd]
        silu = gate * (0.5 + 0.5 * jnp.tanh(0.5 * gate))
        a = (o * _rms_scale(o) * ng_ref[...]).astype(BF16) * silu
        part = _dot(a, w_ref[hd * width:(hd + 1) * width, :])
        y = part if y is None else y + part
    _finish_outproj(y, pg_ref, x_ref, o_ref)


def _outproj(a, w, post_g, x, *, gate=None, norm_g=None, tm=512, kc=256):
    heads, m, width = a.shape
    kdim, d = w.shape
    a_spec = pl.BlockSpec((heads, tm, width), lambda i: (0, i, 0))
    tail_specs = [
        pl.BlockSpec((kdim, d), lambda i: (0, 0)),
        pl.BlockSpec((1, d), lambda i: (0, 0)),
        pl.BlockSpec((tm, d), lambda i: (i, 0)),
    ]
    if gate is None:
        body, in_specs, args = functools.partial(_outproj_kernel, kc=kc), [a_spec], (a,)
    else:
        in_specs = [a_spec, pl.BlockSpec((heads, tm, width), lambda i: (1, i, 0)),
                    pl.BlockSpec((1, width), lambda i: (0, 0))]
        body, args = _gla_outproj_kernel, (a, gate, norm_g)
    return pl.pallas_call(
        body,
        grid=(m // tm,),
        in_specs=in_specs + tail_specs,
        out_specs=pl.BlockSpec((tm, d), lambda i: (i, 0)),
        out_shape=jax.ShapeDtypeStruct((m, d), F32),
        compiler_params=_params("parallel"),
        name="outproj",
    )(*args, w, post_g, x)


def _diff_inproj_kernel(x_ref, g_ref, w_ref, cos_ref, sin_ref, o_ref, vt_ref, *, tn):
    x = x_ref[...]
    h = (x * _rms_scale(x) * g_ref[...]).astype(BF16)
    cos_k, sin_k = cos_ref[...], sin_ref[...]
    cos_q, sin_q = cos_k * DIFF_Q_SCALE, sin_k * DIFF_Q_SCALE
    n_q = DIFF_BRANCH // tn
    per = tn // LANES
    tk = DIFF_KBLOCK
    for c in range(w_ref.shape[1] // tn):
        y = _dot(h, w_ref[:, c * tn:(c + 1) * tn])
        cos, sin = (cos_q, sin_q) if c < n_q else (cos_k, sin_k)
        for s in range(per):
            ysl = y[:, s * LANES:(s + 1) * LANES]
            if c < 2 * n_q:
                ysl = ysl * cos + pltpu.roll(ysl, LANES // 2, 1) * sin
                o_ref[c * per + s] = ysl.astype(o_ref.dtype)
            elif c < 3 * n_q:
                yt = ysl.T.astype(vt_ref.dtype)
                for r in range(yt.shape[1] // tk):
                    vt_ref[(c - 2 * n_q) * per + s, r] = yt[:, r * tk:(r + 1) * tk]
            else:
                o_ref[(c - n_q) * per + s] = ysl.astype(o_ref.dtype)


def _diff_inproj(x, g, w, cos, sin, *, seq, tm=512, tn=512):
    m, d = x.shape
    n = w.shape[1]
    tblocks = seq // tm
    return pl.pallas_call(
        functools.partial(_diff_inproj_kernel, tn=tn),
        grid=(m // tm,),
        in_specs=[
            pl.BlockSpec((tm, d), lambda i: (i, 0)),
            _resident((1, d)),
            _resident((d, n)),
            pl.BlockSpec((tm, LANES), lambda i: (i % tblocks, 0)),
            pl.BlockSpec((tm, LANES), lambda i: (i % tblocks, 0)),
        ],
        out_specs=[
            pl.BlockSpec((3 * DIFF_HEADS, tm, LANES), lambda i: (0, i, 0)),
            pl.BlockSpec((DIFF_HEADS, tm // DIFF_KBLOCK, LANES, DIFF_KBLOCK), lambda i: (0, i, 0, 0)),
        ],
        out_shape=[
            jax.ShapeDtypeStruct((3 * DIFF_HEADS, m, LANES), BF16),
            jax.ShapeDtypeStruct((DIFF_HEADS, m // DIFF_KBLOCK, LANES, DIFF_KBLOCK), BF16),
        ],
        compiler_params=_params("parallel"),
        name="diff_inproj",
    )(x, g, w, cos, sin)


def _rotary_weight_kernel(w_ref, o_ref, *, n_rot):
    quarter = LANES // 4

    @pl.when(pl.program_id(0) < n_rot)
    def _():
        lane = lax.broadcasted_iota(jnp.int32, (w_ref.shape[0], LANES), 1)
        for s in range(w_ref.shape[1] // LANES):
            cols = slice(s * LANES, (s + 1) * LANES)
            w = w_ref[:, cols]
            from_right = pltpu.roll(w, LANES - quarter, 1)
            from_left = pltpu.roll(w, quarter, 1)
            w = jnp.where(lane // quarter == 1, from_right, jnp.where(lane // quarter == 2, from_left, w))
            o_ref[:, cols] = w.astype(o_ref.dtype)

    @pl.when(pl.program_id(0) >= n_rot)
    def _():
        o_ref[...] = w_ref[...].astype(o_ref.dtype)


def _rotary_weight(w, *, tn=512):
    d, n = w.shape
    return pl.pallas_call(
        functools.partial(_rotary_weight_kernel, n_rot=2 * DIFF_BRANCH // tn),
        grid=(n // tn,),
        in_specs=[pl.BlockSpec((d, tn), lambda c: (0, c))],
        out_specs=pl.BlockSpec((d, tn), lambda c: (0, c)),
        out_shape=jax.ShapeDtypeStruct((d, n), BF16),
        compiler_params=_params("parallel"),
        name="diff_wprep",
    )(w)


def _rope_tables(seq):
    d = DIFF_HEAD_DIM
    step = 64
    inv_freq = 1.0 / (ROPE_THETA ** (jnp.arange(0, d, 2, dtype=F32) / d))
    lo = jnp.arange(step, dtype=F32)[:, None] * inv_freq[None, :]
    hi = (jnp.arange(seq // step, dtype=F32) * step)[:, None] * inv_freq[None, :]
    cl, sl, ch, sh = jnp.cos(lo)[None], jnp.sin(lo)[None], jnp.cos(hi)[:, None], jnp.sin(hi)[:, None]
    cos = (ch * cl - sh * sl).reshape(seq, d // 2)
    sin = (sh * cl + ch * sl).reshape(seq, d // 2)
    cos_l = jnp.concatenate([cos, cos, cos, cos], axis=-1)
    sin_l = jnp.concatenate([-sin, -sin, sin, sin], axis=-1)
    return cos_l, sin_l


def _diff_attn_kernel(lq1_ref, lk1_ref, lq2_ref, lk2_ref, q_ref, k_ref, v_ref, gate_ref, ng_ref, o_ref,
                      vt_scr, qs_scr, s_scr, pm_scr, p_scr, m_scr, alpha_scr, acc_scr, *, tq, lambda_init):
    seq = q_ref.shape[0]
    d = DIFF_HEAD_DIM
    tk = DIFF_KBLOCK
    unroll = tq // tk
    maps = (slice(0, tq), slice(tq, 2 * tq))

    ones_row = lax.broadcasted_iota(jnp.int32, (DIFF_ONES_ROWS, tk), 0) == 0
    for r in range(seq // tk):
        vt_scr[r, 0:2 * d, :] = v_ref[r]
        vt_scr[r, 2 * d:, :] = jnp.where(ones_row, 1.0, 0.0).astype(BF16)

    lam = (jnp.exp(jnp.sum(lq1_ref[...] * lk1_ref[...])) - jnp.exp(jnp.sum(lq2_ref[...] * lk2_ref[...]))
           + lambda_init)

    def scores(kstart, cs):
        return _dot(k_ref[pl.ds(kstart, tk), :], qs_scr[:, cs])

    def causal(s):
        kpos = lax.broadcasted_iota(jnp.int32, s.shape, 0)
        qpos = lax.broadcasted_iota(jnp.int32, s.shape, 1)
        return jnp.where(kpos <= qpos, s, -jnp.inf)

    def sublane_max(s):
        return jnp.max(s.reshape(s.shape[0] // SUBLANES, SUBLANES, s.shape[1]), axis=0)

    def softmax_step(s, cs, part_max=None):
        m_old = m_scr[:, cs]
        m_new = jnp.maximum(m_old, jnp.max(s if part_max is None else part_max, axis=0, keepdims=True))
        alpha = jnp.exp2(m_old - m_new)
        p = jnp.exp2(s - m_new)
        m_scr[:, cs] = m_new
        alpha_scr[:, cs] = alpha
        return p.astype(BF16)

    def value_step(p, vt, cs):
        acc_scr[:, cs] = alpha_scr[:, cs] * acc_scr[:, cs] + _dot(vt, p)

    def full_body(jj, carry):
        for u in range(unroll):
            j = jj * unroll + u
            vt_prev = vt_scr[jnp.maximum(j - 1, 0)]
            knext = pl.multiple_of((j + 1) * tk, tk)
            for mp in range(2):
                value_step(p_scr[mp], vt_prev, maps[mp])
                s_next = scores(knext, maps[mp])
                pm_next = sublane_max(s_next)
                p_scr[mp] = softmax_step(s_scr[mp], maps[mp], pm_scr[mp])
                s_scr[mp] = s_next
                pm_scr[mp] = pm_next
        return carry

    for qi in range(seq // tq):
        rows = slice(qi * tq, (qi + 1) * tq)
        qt = q_ref[rows, :].astype(F32).T
        row = lax.broadcasted_iota(jnp.int32, qt.shape, 0)
        is_a = (row // (d // 2)) % 2 == 0
        qs_scr[:, maps[0]] = jnp.where(is_a, qt, 0.0).astype(BF16)
        qs_scr[:, maps[1]] = jnp.where(is_a, 0.0, qt).astype(BF16)
        m_scr[...] = jnp.full_like(m_scr, -jnp.inf)
        acc_scr[...] = jnp.zeros_like(acc_scr)
        p_scr[...] = jnp.zeros_like(p_scr)
        alpha_scr[...] = jnp.ones_like(alpha_scr)

        for mp in range(2):
            s_first = scores(0, maps[mp])
            s_scr[mp] = s_first
            pm_scr[mp] = sublane_max(s_first)
        n_full = qi * unroll
        lax.fori_loop(0, qi, full_body, 0)

        d0 = qi * tq
        cols = [[slice(mp * tq + r * tk, (mp + 1) * tq) for mp in range(2)] for r in range(unroll)]
        s_cur = [s_scr[mp] for mp in range(2)]
        p_prev = [p_scr[mp] for mp in range(2)]
        vt_prev = vt_scr[max(n_full - 1, 0)]
        cols_prev = maps
        for r in range(unroll):
            s_nxt = [None, None]
            for mp in range(2):
                value_step(p_prev[mp], vt_prev, cols_prev[mp])
                if r + 1 < unroll:
                    s_nxt[mp] = scores(d0 + (r + 1) * tk, cols[r + 1][mp])
                p_prev[mp] = softmax_step(causal(s_cur[mp]), cols[r][mp])
            s_cur, vt_prev, cols_prev = s_nxt, vt_scr[n_full + r], cols[r]
        for mp in range(2):
            value_step(p_prev[mp], vt_prev, cols_prev[mp])

        inv_l = 1.0 / acc_scr[2 * d:2 * d + 1, :]
        o_t = (acc_scr[0:2 * d, maps[0]] * inv_l[:, maps[0]]
               - acc_scr[0:2 * d, maps[1]] * (lam * inv_l[:, maps[1]]))
        ms = jnp.sum(o_t * o_t, axis=0, keepdims=True) * (1.0 / (2 * d))
        o = (o_t * (lax.rsqrt(ms + EPS) * (1.0 - lambda_init))).T
        gate = gate_ref[rows, :]
        silu = gate * (0.5 + 0.5 * jnp.tanh(0.5 * gate))
        o_ref[rows, :] = (o * ng_ref[...]).astype(o_ref.dtype) * silu


def _diff_attn(proj, vt, lq1, lk1, lq2, lk2, norm_g, lambda_init, *, batch, seq, tq=1024):
    m = proj.shape[1]
    hd = 2 * DIFF_HEAD_DIM
    nkb = seq // DIFF_KBLOCK
    vec = pl.BlockSpec((1, DIFF_HEAD_DIM), lambda b, h: (0, 0))
    head_rows = lambda part: pl.BlockSpec((None, seq, hd), lambda b, h: (part * DIFF_HEADS + h, b, 0))
    return pl.pallas_call(
        functools.partial(_diff_attn_kernel, tq=tq, lambda_init=lambda_init),
        grid=(batch, DIFF_HEADS),
        in_specs=[vec, vec, vec, vec, head_rows(0), head_rows(1),
                  pl.BlockSpec((None, nkb, hd, DIFF_KBLOCK), lambda b, h: (h, b, 0, 0)),
                  head_rows(2),
                  pl.BlockSpec((1, hd), lambda b, h: (0, 0))],
        out_specs=pl.BlockSpec((None, seq, hd), lambda b, h: (h, b, 0)),
        out_shape=jax.ShapeDtypeStruct((DIFF_HEADS, m, hd), BF16),
        scratch_shapes=[
            pltpu.VMEM((seq // DIFF_KBLOCK, hd + DIFF_ONES_ROWS, DIFF_KBLOCK), BF16),
            pltpu.VMEM((hd, 2 * tq), BF16),
            pltpu.VMEM((2, DIFF_KBLOCK, tq), F32),
            pltpu.VMEM((2, SUBLANES, tq), F32),
            pltpu.VMEM((2, DIFF_KBLOCK, tq), BF16),
            pltpu.VMEM((1, 2 * tq), F32),
            pltpu.VMEM((1, 2 * tq), F32),
            pltpu.VMEM((hd + DIFF_ONES_ROWS, 2 * tq), F32),
        ],
        compiler_params=_params("parallel", "parallel"),
        name="diff_attn",
    )(lq1, lk1, lq2, lk2, proj, proj, vt, proj, norm_g)


def _gla_layer(x, pre_g, post_g, w_in, w_g2, b_g, norm_g, w_out, *, batch, seq):
    qkk, vg, eb = _gla_inproj(x, pre_g[None, :], w_in.astype(BF16), w_g2.astype(BF16), b_g[None, :])
    og = _gla_core(qkk, vg, eb, batch=batch, seq=seq)
    return _outproj(og, w_out.astype(BF16), post_g[None, :], x, gate=vg, norm_g=norm_g[None, :])


def _diff_layer(x, pre_g, post_g, w_in, lq1, lk1, lq2, lk2, norm_g, w_out, lambda_init, *, batch, seq):
    cos, sin = _rope_tables(seq)
    proj, vt = _diff_inproj(x, pre_g[None, :], _rotary_weight(w_in), cos, sin, seq=seq)
    og = _diff_attn(proj, vt, lq1[None, :], lk1[None, :], lq2[None, :], lk2[None, :], norm_g[None, :],
                    lambda_init, batch=batch, seq=seq)
    return _outproj(og, w_out.astype(BF16), post_g[None, :], x)


def kernel(x, pre_g, post_g, gla_w_in, gla_w_g2, gla_b_g, gla_norm_g, gla_w_out, diff_w_in,
           diff_lam_q1, diff_lam_k1, diff_lam_q2, diff_lam_k2, diff_norm_g, diff_w_out):
    batch, seq, d_model = x.shape
    depth = pre_g.shape[0]
    xf = x.reshape(batch * seq, d_model)
    for i in range(depth):
        j = i // 2
        if i % 2 == 0:
            xf = _gla_layer(xf, pre_g[i], post_g[i], gla_w_in[j], gla_w_g2[j], gla_b_g[j], gla_norm_g[j],
                            gla_w_out[j], batch=batch, seq=seq)
        else:
            lambda_init = 0.8 - 0.6 * math.exp(-0.3 * i)
            xf = _diff_layer(xf, pre_g[i], post_g[i], diff_w_in[j], diff_lam_q1[j], diff_lam_k1[j],
                             diff_lam_q2[j], diff_lam_k2[j], diff_norm_g[j], diff_w_out[j], lambda_init,
                             batch=batch, seq=seq)
    return xf.reshape(batch, seq, d_model)
```

```python
import functools
import math

import jax
import jax.numpy as jnp
from jax import lax
from jax.experimental import pallas as pl
from jax.experimental.pallas import tpu as pltpu

F32 = jnp.float32
BF16 = jnp.bfloat16

EPS = 1e-6
LANES = 128
SUBLANES = 8

GLA_HEADS = 4
GLA_DK = 128
GLA_DV = 512
GLA_RANK = 16
GLA_GATE_NORM = 16.0
GLA_CHUNK = 64
GLA_CUM_ROWS = 256
GLA_QK = GLA_HEADS * GLA_DK
GLA_BRANCH = GLA_HEADS * GLA_DV
GLA_MAIN = 2 * GLA_QK + 2 * GLA_BRANCH

DIFF_HEAD_DIM = 64
DIFF_HEADS = 16
DIFF_BRANCH = DIFF_HEADS * 2 * DIFF_HEAD_DIM
DIFF_KBLOCK = 256
DIFF_ONES_ROWS = 16
DIFF_Q_SCALE = DIFF_HEAD_DIM ** -0.5 * math.log2(math.e)
ROPE_THETA = 10000.0

VMEM_LIMIT = 48 * 1024 * 1024


def _dot(a, b):
    return jnp.dot(a, b, preferred_element_type=F32)


def _dot_nt(a, b):
    return lax.dot_general(a, b, (((1,), (1,)), ((), ())), preferred_element_type=F32)


def _dot_tn(a, b):
    return lax.dot_general(a, b, (((0,), (0,)), ((), ())), preferred_element_type=F32)


def _rms_scale(x):
    return lax.rsqrt(jnp.mean(x * x, axis=-1, keepdims=True) + EPS)


def _params(*sem):
    return pltpu.CompilerParams(dimension_semantics=sem, vmem_limit_bytes=VMEM_LIMIT)


def _gla_inproj_kernel(x_ref, g_ref, w_ref, wg2_ref, bg_ref, qkk_ref, vg_ref, eb_ref):
    tm = x_ref.shape[0]
    c = GLA_CHUNK
    sub = GLA_CUM_ROWS
    x = x_ref[...]
    h = (x * _rms_scale(x) * g_ref[...]).astype(BF16)
    low = _dot(h, w_ref[:, GLA_MAIN:GLA_MAIN + GLA_RANK])
    z = _dot(low.astype(BF16), wg2_ref[...]) + bg_ref[...]
    logg = (jnp.minimum(z, 0.0) - jnp.log(1.0 + jnp.exp(-jnp.abs(z)))) / GLA_GATE_NORM

    row = lax.broadcasted_iota(jnp.int32, (sub, sub), 0)
    col = lax.broadcasted_iota(jnp.int32, (sub, sub), 1)
    tri = jnp.where((row >= col) & (row // c == col // c), 1.0, 0.0).astype(BF16)
    lg_hi = logg.astype(BF16)
    lg_lo = (logg - lg_hi.astype(F32)).astype(BF16)
    split = jnp.concatenate([lg_hi, lg_lo], axis=1)
    cum = jnp.concatenate([_dot(tri, split[r * sub:(r + 1) * sub, :]) for r in range(tm // sub)], axis=0)
    bcum = cum[:, :GLA_QK] + cum[:, GLA_QK:]
    blast = [bcum[(n + 1) * c - 1:(n + 1) * c, :] for n in range(tm // c)]
    blast_rows = jnp.concatenate([jnp.broadcast_to(bl, (c, GLA_QK)) for bl in blast], axis=0)
    eb = jnp.exp(jnp.concatenate(blast, axis=0))

    q = _dot(h, w_ref[:, 0:GLA_QK]) * (GLA_DK ** -0.5)
    k = _dot(h, w_ref[:, GLA_QK:2 * GLA_QK])
    parts = (q * jnp.exp(bcum), k * jnp.exp(-bcum), k * jnp.exp(blast_rows - bcum))
    for i, part in enumerate(parts):
        part = part.astype(qkk_ref.dtype)
        for hd in range(GLA_HEADS):
            qkk_ref[i * GLA_HEADS + hd] = part[:, hd * GLA_DK:(hd + 1) * GLA_DK]
    for hd in range(GLA_HEADS):
        eb_ref[hd] = eb[:, hd * GLA_DK:(hd + 1) * GLA_DK]
    for cc in range(2 * GLA_HEADS):
        c0 = 2 * GLA_QK + cc * GLA_DV
        vg_ref[cc] = _dot(h, w_ref[:, c0:c0 + GLA_DV]).astype(vg_ref.dtype)


def _resident(shape):
    return pl.BlockSpec(shape, lambda i: (0,) * len(shape), pipeline_mode=pl.Buffered(1))


def _gla_inproj(x, g, w, w_g2, b_g, *, tm=512):
    m, d = x.shape
    return pl.pallas_call(
        _gla_inproj_kernel,
        grid=(m // tm,),
        in_specs=[
            pl.BlockSpec((tm, d), lambda i: (i, 0)),
            _resident((1, d)),
            _resident(w.shape),
            _resident(w_g2.shape),
            _resident((1, GLA_QK)),
        ],
        out_specs=[
            pl.BlockSpec((3 * GLA_HEADS, tm, GLA_DK), lambda i: (0, i, 0)),
            pl.BlockSpec((2 * GLA_HEADS, tm, GLA_DV), lambda i: (0, i, 0)),
            pl.BlockSpec((GLA_HEADS, tm // GLA_CHUNK, GLA_DK), lambda i: (0, i, 0)),
        ],
        out_shape=[
            jax.ShapeDtypeStruct((3 * GLA_HEADS, m, GLA_DK), BF16),
            jax.ShapeDtypeStruct((2 * GLA_HEADS, m, GLA_DV), BF16),
            jax.ShapeDtypeStruct((GLA_HEADS, m // GLA_CHUNK, GLA_DK), F32),
        ],
        compiler_params=_params("parallel"),
        name="gla_inproj",
    )(x, g, w, w_g2, b_g)


def _gla_core_kernel(qt_ref, kt_ref, ke_ref, v_ref, eb_ref, o_ref, st_scr, oi_scr, kv_scr, *, tc):
    seq = qt_ref.shape[0]
    c = GLA_CHUNK
    nchunk = tc // c
    nb = seq // tc
    chunks = [slice(n * c, (n + 1) * c) for n in range(nchunk)]
    row = lax.broadcasted_iota(jnp.int32, (tc, tc), 0)
    col = lax.broadcasted_iota(jnp.int32, (tc, tc), 1)
    causal = (row >= col) & (row // c == col // c)
    own_chunk = (lax.broadcasted_iota(jnp.int32, (tc, nchunk * GLA_DK), 0) // c
                 == lax.broadcasted_iota(jnp.int32, (tc, nchunk * GLA_DK), 1) // GLA_DK)

    def block_rows(t):
        return pl.ds(pl.multiple_of(t * tc, tc), tc)

    def products(t):
        slot = t % 2
        rows = block_rows(t)
        v = v_ref[rows, :]
        a = jnp.where(causal, _dot_nt(qt_ref[rows, :], kt_ref[rows, :]), 0.0).astype(BF16)
        oi_scr[slot] = _dot(a, v)
        ke = jnp.concatenate([ke_ref[rows, :]] * nchunk, axis=1)
        ke = jnp.where(own_chunk, ke, jnp.zeros_like(ke))
        kv_scr[slot] = _dot_tn(v, ke)

    def recur(t):
        slot = t % 2
        st = st_scr[...]
        for n, sl in enumerate(chunks):
            rows = pl.ds(pl.multiple_of(t * tc + n * c, c), c)
            o = oi_scr[slot, sl, :] + _dot_nt(qt_ref[rows, :], st.astype(BF16))
            o_ref[rows, :] = o.astype(o_ref.dtype)
            st = st * eb_ref[pl.ds(t * nchunk + n, 1), :] + kv_scr[slot, :, n * GLA_DK:(n + 1) * GLA_DK]
        st_scr[...] = st

    st_scr[...] = jnp.zeros_like(st_scr)
    products(0)

    def body(t, carry):
        recur(t)
        products(t + 1)
        return carry

    lax.fori_loop(0, nb - 1, body, 0)
    recur(nb - 1)


def _gla_core(qkk, vg, eb, *, batch, seq, tc=256):
    m = qkk.shape[1]
    nchunk = tc // GLA_CHUNK
    head_rows = lambda part, width: pl.BlockSpec((None, seq, width), lambda b, h: (part * GLA_HEADS + h, b, 0))
    return pl.pallas_call(
        functools.partial(_gla_core_kernel, tc=tc),
        grid=(batch, GLA_HEADS),
        in_specs=[head_rows(0, GLA_DK), head_rows(1, GLA_DK), head_rows(2, GLA_DK), head_rows(0, GLA_DV),
                  pl.BlockSpec((None, seq // GLA_CHUNK, GLA_DK), lambda b, h: (h, b, 0))],
        out_specs=pl.BlockSpec((None, seq, GLA_DV), lambda b, h: (h, b, 0)),
        out_shape=jax.ShapeDtypeStruct((GLA_HEADS, m, GLA_DV), BF16),
        scratch_shapes=[
            pltpu.VMEM((GLA_DV, GLA_DK), F32),
            pltpu.VMEM((2, tc, GLA_DV), F32),
            pltpu.VMEM((2, GLA_DV, nchunk * GLA_DK), F32),
        ],
        compiler_params=_params("parallel", "parallel"),
        name="gla_core",
    )(qkk, qkk, qkk, vg, eb)


def _finish_outproj(y, pg_ref, x_ref, o_ref):
    o_ref[...] = x_ref[...] + y * _rms_scale(y) * pg_ref[...]


def _outproj_kernel(a_ref, w_ref, pg_ref, x_ref, o_ref, *, kc):
    heads, _, width = a_ref.shape
    per = kc // width
    y = None
    for j in range(heads // per):
        a = jnp.concatenate([a_ref[j * per + i] for i in range(per)], axis=1)
        part = _dot(a, w_ref[j * kc:(j + 1) * kc, :])
        y = part if y is None else y + part
    _finish_outproj(y, pg_ref, x_ref, o_ref)


def _gla_outproj_kernel(a_ref, gate_ref, ng_ref, w_ref, pg_ref, x_ref, o_ref):
    heads, _, width = a_ref.shape
    y = None
    for hd in range(heads):
        o = a_ref[hd].astype(F32)
        gate = gate_ref[hd]
        silu = gate * (0.5 + 0.5 * jnp.tanh(0.5 * gate))
        a = (o * _rms_scale(o) * ng_ref[...]).astype(BF16) * silu
        part = _dot(a, w_ref[hd * width:(hd + 1) * width, :])
        y = part if y is None else y + part
    _finish_outproj(y, pg_ref, x_ref, o_ref)


def _outproj(a, w, post_g, x, *, gate=None, norm_g=None, tm=512, kc=256):
    heads, m, width = a.shape
    kdim, d = w.shape
    a_spec = pl.BlockSpec((heads, tm, width), lambda i: (0, i, 0))
    tail_specs = [
        pl.BlockSpec((kdim, d), lambda i: (0, 0)),
        pl.BlockSpec((1, d), lambda i: (0, 0)),
        pl.BlockSpec((tm, d), lambda i: (i, 0)),
    ]
    if gate is None:
        body, in_specs, args = functools.partial(_outproj_kernel, kc=kc), [a_spec], (a,)
    else:
        in_specs = [a_spec, pl.BlockSpec((heads, tm, width), lambda i: (1, i, 0)),
                    pl.BlockSpec((1, width), lambda i: (0, 0))]
        body, args = _gla_outproj_kernel, (a, gate, norm_g)
    return pl.pallas_call(
        body,
        grid=(m // tm,),
        in_specs=in_specs + tail_specs,
        out_specs=pl.BlockSpec((tm, d), lambda i: (i, 0)),
        out_shape=jax.ShapeDtypeStruct((m, d), F32),
        compiler_params=_params("parallel"),
        name="outproj",
    )(*args, w, post_g, x)


def _diff_inproj_kernel(x_ref, g_ref, w_ref, cos_ref, sin_ref, o_ref, vt_ref, *, tn):
    x = x_ref[...]
    h = (x * _rms_scale(x) * g_ref[...]).astype(BF16)
    cos_k, sin_k = cos_ref[...], sin_ref[...]
    cos_q, sin_q = cos_k * DIFF_Q_SCALE, sin_k * DIFF_Q_SCALE
    n_q = DIFF_BRANCH // tn
    per = tn // LANES
    tk = DIFF_KBLOCK
    for c in range(w_ref.shape[1] // tn):
        y = _dot(h, w_ref[:, c * tn:(c + 1) * tn])
        cos, sin = (cos_q, sin_q) if c < n_q else (cos_k, sin_k)
        for s in range(per):
            ysl = y[:, s * LANES:(s + 1) * LANES]
            if c < 2 * n_q:
                ysl = ysl * cos + pltpu.roll(ysl, LANES // 2, 1) * sin
                o_ref[c * per + s] = ysl.astype(o_ref.dtype)
            elif c < 3 * n_q:
                yt = ysl.T.astype(vt_ref.dtype)
                for r in range(yt.shape[1] // tk):
                    vt_ref[(c - 2 * n_q) * per + s, r] = yt[:, r * tk:(r + 1) * tk]
            else:
                o_ref[(c - n_q) * per + s] = ysl.astype(o_ref.dtype)


def _diff_inproj(x, g, w, cos, sin, *, seq, tm=512, tn=512):
    m, d = x.shape
    n = w.shape[1]
    tblocks = seq // tm
    return pl.pallas_call(
        functools.partial(_diff_inproj_kernel, tn=tn),
        grid=(m // tm,),
        in_specs=[
            pl.BlockSpec((tm, d), lambda i: (i, 0)),
            _resident((1, d)),
            _resident((d, n)),
            pl.BlockSpec((tm, LANES), lambda i: (i % tblocks, 0)),
            pl.BlockSpec((tm, LANES), lambda i: (i % tblocks, 0)),
        ],
        out_specs=[
            pl.BlockSpec((3 * DIFF_HEADS, tm, LANES), lambda i: (0, i, 0)),
            pl.BlockSpec((DIFF_HEADS, tm // DIFF_KBLOCK, LANES, DIFF_KBLOCK), lambda i: (0, i, 0, 0)),
        ],
        out_shape=[
            jax.ShapeDtypeStruct((3 * DIFF_HEADS, m, LANES), BF16),
            jax.ShapeDtypeStruct((DIFF_HEADS, m // DIFF_KBLOCK, LANES, DIFF_KBLOCK), BF16),
        ],
        compiler_params=_params("parallel"),
        name="diff_inproj",
    )(x, g, w, cos, sin)


def _rotary_weight_kernel(w_ref, o_ref, *, n_rot):
    quarter = LANES // 4

    @pl.when(pl.program_id(0) < n_rot)
    def _():
        lane = lax.broadcasted_iota(jnp.int32, (w_ref.shape[0], LANES), 1)
        for s in range(w_ref.shape[1] // LANES):
            cols = slice(s * LANES, (s + 1) * LANES)
            w = w_ref[:, cols]
            from_right = pltpu.roll(w, LANES - quarter, 1)
            from_left = pltpu.roll(w, quarter, 1)
            w = jnp.where(lane // quarter == 1, from_right, jnp.where(lane // quarter == 2, from_left, w))
            o_ref[:, cols] = w.astype(o_ref.dtype)

    @pl.when(pl.program_id(0) >= n_rot)
    def _():
        o_ref[...] = w_ref[...].astype(o_ref.dtype)


def _rotary_weight(w, *, tn=512):
    d, n = w.shape
    return pl.pallas_call(
        functools.partial(_rotary_weight_kernel, n_rot=2 * DIFF_BRANCH // tn),
        grid=(n // tn,),
        in_specs=[pl.BlockSpec((d, tn), lambda c: (0, c))],
        out_specs=pl.BlockSpec((d, tn), lambda c: (0, c)),
        out_shape=jax.ShapeDtypeStruct((d, n), BF16),
        compiler_params=_params("parallel"),
        name="diff_wprep",
    )(w)


def _rope_tables(seq):
    d = DIFF_HEAD_DIM
    step = 64
    inv_freq = 1.0 / (ROPE_THETA ** (jnp.arange(0, d, 2, dtype=F32) / d))
    lo = jnp.arange(step, dtype=F32)[:, None] * inv_freq[None, :]
    hi = (jnp.arange(seq // step, dtype=F32) * step)[:, None] * inv_freq[None, :]
    cl, sl, ch, sh = jnp.cos(lo)[None], jnp.sin(lo)[None], jnp.cos(hi)[:, None], jnp.sin(hi)[:, None]
    cos = (ch * cl - sh * sl).reshape(seq, d // 2)
    sin = (sh * cl + ch * sl).reshape(seq, d // 2)
    cos_l = jnp.concatenate([cos, cos, cos, cos], axis=-1)
    sin_l = jnp.concatenate([-sin, -sin, sin, sin], axis=-1)
    return cos_l, sin_l


def _diff_attn_kernel(lq1_ref, lk1_ref, lq2_ref, lk2_ref, q_ref, k_ref, v_ref, gate_ref, ng_ref, o_ref,
                      vt_scr, qs_scr, s_scr, pm_scr, p_scr, m_scr, alpha_scr, acc_scr, *, tq, lambda_init):
    seq = q_ref.shape[0]
    d = DIFF_HEAD_DIM
    tk = DIFF_KBLOCK
    unroll = tq // tk
    maps = (slice(0, tq), slice(tq, 2 * tq))

    ones_row = lax.broadcasted_iota(jnp.int32, (DIFF_ONES_ROWS, tk), 0) == 0
    for r in range(seq // tk):
        vt_scr[r, 0:2 * d, :] = v_ref[r]
        vt_scr[r, 2 * d:, :] = jnp.where(ones_row, 1.0, 0.0).astype(BF16)

    lam = (jnp.exp(jnp.sum(lq1_ref[...] * lk1_ref[...])) - jnp.exp(jnp.sum(lq2_ref[...] * lk2_ref[...]))
           + lambda_init)

    def scores(kstart, cs):
        return _dot(k_ref[pl.ds(kstart, tk), :], qs_scr[:, cs])

    def causal(s):
        kpos = lax.broadcasted_iota(jnp.int32, s.shape, 0)
        qpos = lax.broadcasted_iota(jnp.int32, s.shape, 1)
        return jnp.where(kpos <= qpos, s, -jnp.inf)

    def sublane_max(s):
        return jnp.max(s.reshape(s.shape[0] // SUBLANES, SUBLANES, s.shape[1]), axis=0)

    def softmax_step(s, cs, part_max=None):
        m_old = m_scr[:, cs]
        m_new = jnp.maximum(m_old, jnp.max(s if part_max is None else part_max, axis=0, keepdims=True))
        alpha = jnp.exp2(m_old - m_new)
        p = jnp.exp2(s - m_new)
        m_scr[:, cs] = m_new
        alpha_scr[:, cs] = alpha
        return p.astype(BF16)

    def value_step(p, vt, cs):
        acc_scr[:, cs] = alpha_scr[:, cs] * acc_scr[:, cs] + _dot(vt, p)

    def full_body(jj, carry):
        for u in range(unroll):
            j = jj * unroll + u
            vt_prev = vt_scr[jnp.maximum(j - 1, 0)]
            knext = pl.multiple_of((j + 1) * tk, tk)
            for mp in range(2):
                value_step(p_scr[mp], vt_prev, maps[mp])
                s_next = scores(knext, maps[mp])
                pm_next = sublane_max(s_next)
                p_scr[mp] = softmax_step(s_scr[mp], maps[mp], pm_scr[mp])
                s_scr[mp] = s_next
                pm_scr[mp] = pm_next
        return carry

    for qi in range(seq // tq):
        rows = slice(qi * tq, (qi + 1) * tq)
        qt = q_ref[rows, :].astype(F32).T
        row = lax.broadcasted_iota(jnp.int32, qt.shape, 0)
        is_a = (row // (d // 2)) % 2 == 0
        qs_scr[:, maps[0]] = jnp.where(is_a, qt, 0.0).astype(BF16)
        qs_scr[:, maps[1]] = jnp.where(is_a, 0.0, qt).astype(BF16)
        m_scr[...] = jnp.full_like(m_scr, -jnp.inf)
        acc_scr[...] = jnp.zeros_like(acc_scr)
        p_scr[...] = jnp.zeros_like(p_scr)
        alpha_scr[...] = jnp.ones_like(alpha_scr)

        for mp in range(2):
            s_first = scores(0, maps[mp])
            s_scr[mp] = s_first
            pm_scr[mp] = sublane_max(s_first)
        n_full = qi * unroll
        lax.fori_loop(0, qi, full_body, 0)

        d0 = qi * tq
        cols = [[slice(mp * tq + r * tk, (mp + 1) * tq) for mp in range(2)] for r in range(unroll)]
        s_cur = [s_scr[mp] for mp in range(2)]
        p_prev = [p_scr[mp] for mp in range(2)]
        vt_prev = vt_scr[max(n_full - 1, 0)]
        cols_prev = maps
        for r in range(unroll):
            s_nxt = [None, None]
            for mp in range(2):
                value_step(p_prev[mp], vt_prev, cols_prev[mp])
                if r + 1 < unroll:
                    s_nxt[mp] = scores(d0 + (r + 1) * tk, cols[r + 1][mp])
                p_prev[mp] = softmax_step(causal(s_cur[mp]), cols[r][mp])
            s_cur, vt_prev, cols_prev = s_nxt, vt_scr[n_full + r], cols[r]
        for mp in range(2):
            value_step(p_prev[mp], vt_prev, cols_prev[mp])

        inv_l = 1.0 / acc_scr[2 * d:2 * d + 1, :]
        o_t = (acc_scr[0:2 * d, maps[0]] * inv_l[:, maps[0]]
               - acc_scr[0:2 * d, maps[1]] * (lam * inv_l[:, maps[1]]))
        ms = jnp.sum(o_t * o_t, axis=0, keepdims=True) * (1.0 / (2 * d))
        o = (o_t * (lax.rsqrt(ms + EPS) * (1.0 - lambda_init))).T
        gate = gate_ref[rows, :]
        silu = gate * (0.5 + 0.5 * jnp.tanh(0.5 * gate))
        o_ref[rows, :] = (o * ng_ref[...]).astype(o_ref.dtype) * silu


def _diff_attn(proj, vt, lq1, lk1, lq2, lk2, norm_g, lambda_init, *, batch, seq, tq=1024):
    m = proj.shape[1]
    hd = 2 * DIFF_HEAD_DIM
    nkb = seq // DIFF_KBLOCK
    vec = pl.BlockSpec((1, DIFF_HEAD_DIM), lambda b, h: (0, 0))
    head_rows = lambda part: pl.BlockSpec((None, seq, hd), lambda b, h: (part * DIFF_HEADS + h, b, 0))
    return pl.pallas_call(
        functools.partial(_diff_attn_kernel, tq=tq, lambda_init=lambda_init),
        grid=(batch, DIFF_HEADS),
        in_specs=[vec, vec, vec, vec, head_rows(0), head_rows(1),
                  pl.BlockSpec((None, nkb, hd, DIFF_KBLOCK), lambda b, h: (h, b, 0, 0)),
                  head_rows(2),
                  pl.BlockSpec((1, hd), lambda b, h: (0, 0))],
        out_specs=pl.BlockSpec((None, seq, hd), lambda b, h: (h, b, 0)),
        out_shape=jax.ShapeDtypeStruct((DIFF_HEADS, m, hd), BF16),
        scratch_shapes=[
            pltpu.VMEM((seq // DIFF_KBLOCK, hd + DIFF_ONES_ROWS, DIFF_KBLOCK), BF16),
            pltpu.VMEM((hd, 2 * tq), BF16),
            pltpu.VMEM((2, DIFF_KBLOCK, tq), F32),
            pltpu.VMEM((2, SUBLANES, tq), F32),
            pltpu.VMEM((2, DIFF_KBLOCK, tq), BF16),
            pltpu.VMEM((1, 2 * tq), F32),
            pltpu.VMEM((1, 2 * tq), F32),
            pltpu.VMEM((hd + DIFF_ONES_ROWS, 2 * tq), F32),
        ],
        compiler_params=_params("parallel", "parallel"),
        name="diff_attn",
    )(lq1, lk1, lq2, lk2, proj, proj, vt, proj, norm_g)


def _gla_layer(x, pre_g, post_g, w_in, w_g2, b_g, norm_g, w_out, *, batch, seq):
    qkk, vg, eb = _gla_inproj(x, pre_g[None, :], w_in.astype(BF16), w_g2.astype(BF16), b_g[None, :])
    og = _gla_core(qkk, vg, eb, batch=batch, seq=seq)
    return _outproj(og, w_out.astype(BF16), post_g[None, :], x, gate=vg, norm_g=norm_g[None, :])


def _diff_layer(x, pre_g, post_g, w_in, lq1, lk1, lq2, lk2, norm_g, w_out, lambda_init, *, batch, seq):
    cos, sin = _rope_tables(seq)
    proj, vt = _diff_inproj(x, pre_g[None, :], _rotary_weight(w_in), cos, sin, seq=seq)
    og = _diff_attn(proj, vt, lq1[None, :], lk1[None, :], lq2[None, :], lk2[None, :], norm_g[None, :],
                    lambda_init, batch=batch, seq=seq)
    return _outproj(og, w_out.astype(BF16), post_g[None, :], x, tm=1024)


def kernel(x, pre_g, post_g, gla_w_in, gla_w_g2, gla_b_g, gla_norm_g, gla_w_out, diff_w_in,
           diff_lam_q1, diff_lam_k1, diff_lam_q2, diff_lam_k2, diff_norm_g, diff_w_out):
    batch, seq, d_model = x.shape
    depth = pre_g.shape[0]
    xf = x.reshape(batch * seq, d_model)
    for i in range(depth):
        j = i // 2
        if i % 2 == 0:
            xf = _gla_layer(xf, pre_g[i], post_g[i], gla_w_in[j], gla_w_g2[j], gla_b_g[j], gla_norm_g[j],
                            gla_w_out[j], batch=batch, seq=seq)
        else:
            lambda_init = 0.8 - 0.6 * math.exp(-0.3 * i)
            xf = _diff_layer(xf, pre_g[i], post_g[i], diff_w_in[j], diff_lam_q1[j], diff_lam_k1[j],
                             diff_lam_q2[j], diff_lam_k2[j], diff_norm_g[j], diff_w_out[j], lambda_init,
                             batch=batch, seq=seq)
    return xf.reshape(batch, seq, d_model)
```
